```python
import math
import jax, jax.numpy as jnp
from jax import lax
import numpy as np

D_MODEL = 1024
BATCH = 2
SEQ = 16384
DEPTH = 2
DEC_BATCH = 16
DEC_SEQ = 64
PAST_LEN = 4096

CHUNK = 64
CONV_W = 4
D_FF = 2816
EPS = 1e-6
A_HEADS = 6
A_DK = 64
A_DV = 64
A_WIDTH = A_HEADS * A_DV
A_CONV_CH = 2 * A_HEADS * A_DK + A_WIDTH
B_HEADS = 6
B_HEADDIM = 64
B_WIDTH = B_HEADS * B_HEADDIM
B_GROUPS = 2
B_STATE = 64
B_CONV_CH = B_WIDTH + 2 * B_GROUPS * B_STATE
C_WIDTH = 256
C_BLOCKS = 8
C_BLOCK = C_WIDTH // C_BLOCKS
LRU_C = 8.0
D_MIX = A_WIDTH + B_WIDTH + C_WIDTH
D_IN = A_CONV_CH + A_WIDTH + 2 * A_HEADS + B_WIDTH + B_CONV_CH + B_HEADS + 2 * C_WIDTH

kernel_name = 'hybrid_streaming_encoder_step'


def rmsnorm(x, w):
    xf = x.astype(jnp.float32)
    y = xf * lax.rsqrt(jnp.mean(xf * xf, axis=-1, keepdims=True) + EPS)
    return (y * w.astype(jnp.float32)).astype(x.dtype)


def l2norm(x):
    return x * lax.rsqrt(jnp.sum(x * x, axis=-1, keepdims=True) + EPS)


def swiglu(x, w_gate, w_up, w_down):
    return (jax.nn.silu(x @ w_gate) * (x @ w_up)) @ w_down


def causal_conv(x, buf, w):
    l = x.shape[1]
    xp = jnp.concatenate([buf.astype(x.dtype), x], axis=1)
    y = xp[:, 0:l] * w[0]
    for k in range(1, CONV_W):
        y = y + xp[:, k:k + l] * w[k]
    return y, xp[:, l:]


def to_chunks(t, n_chunks):
    bsz, l = t.shape[:2]
    t = jnp.pad(t.astype(jnp.float32), [(0, 0), (0, n_chunks * CHUNK - l)] + [(0, 0)] * (t.ndim - 2))
    t = t.reshape((bsz, n_chunks, CHUNK) + t.shape[2:])
    return jnp.moveaxis(jnp.moveaxis(t, 3, 2), 1, 0)


def from_chunks(o, l):
    n, bsz, h, c, e = o.shape
    return jnp.transpose(o, (1, 0, 3, 2, 4)).reshape(bsz, n * c, h, e)[:, :l]


def chunk_masks():
    causal = jnp.tril(jnp.ones((CHUNK, CHUNK), dtype=bool))
    strict = jnp.tril(jnp.ones((CHUNK, CHUNK), dtype=bool), -1)
    return causal, strict


def decay_matrix(g, mask):
    return jnp.exp(jnp.where(mask, g[..., :, None] - g[..., None, :], -jnp.inf))


def gated_delta_rule(q, k, v, log_alpha, beta, s0):
    l = q.shape[1]
    n = -(-l // CHUNK)
    qc, kc, vc = to_chunks(q, n), to_chunks(k, n), to_chunks(v, n)
    la, bc = to_chunks(log_alpha, n), to_chunks(beta, n)
    causal, strict = chunk_masks()
    g = jnp.cumsum(la, axis=-1)
    gam = decay_matrix(g, causal)
    kk = jnp.einsum('nbhid,nbhjd->nbhij', kc, kc)
    a_mat = jnp.eye(CHUNK, dtype=jnp.float32) + jnp.where(strict, bc[..., :, None] * kk * gam, 0.0)
    u = lax.linalg.triangular_solve(a_mat, bc[..., None] * vc, left_side=True, lower=True, unit_diagonal=True)
    w = lax.linalg.triangular_solve(a_mat, (bc * jnp.exp(g))[..., None] * kc, left_side=True, lower=True, unit_diagonal=True)
    qk = jnp.where(causal, jnp.einsum('nbhid,nbhjd->nbhij', qc, kc) * gam, 0.0)
    q_dec = qc * jnp.exp(g)[..., None]
    k_dec = kc * jnp.exp(g[..., -1:] - g)[..., None]
    g_tot = jnp.exp(g[..., -1])

    def step(s, inp):
        u_c, w_c, qk_c, q_c, k_c, gt = inp
        delta = u_c - jnp.einsum('bhik,bhkv->bhiv', w_c, s)
        o = jnp.einsum('bhik,bhkv->bhiv', q_c, s) + jnp.einsum('bhij,bhjv->bhiv', qk_c, delta)
        s = gt[..., None, None] * s + jnp.einsum('bhik,bhiv->bhkv', k_c, delta)
        return s, o

    s_fin, o = lax.scan(step, s0, (u, w, qk, q_dec, k_dec, g_tot))
    return from_chunks(o, l), s_fin


def ssd_scan(x, dt, a, bm, cm, h0):
    l = x.shape[1]
    n = -(-l // CHUNK)
    xdt = to_chunks(x * dt[..., None], n)
    la = to_chunks(dt * a, n)
    bc, cc = to_chunks(bm, n), to_chunks(cm, n)
    causal, _ = chunk_masks()
    g = jnp.cumsum(la, axis=-1)
    scores = jnp.einsum('nbhis,nbhjs->nbhij', cc, bc) * decay_matrix(g, causal)
    y_intra = jnp.einsum('nbhij,nbhjp->nbhip', scores, xdt)
    c_dec = cc * jnp.exp(g)[..., None]
    b_dec = bc * jnp.exp(g[..., -1:] - g)[..., None]
    g_tot = jnp.exp(g[..., -1])

    def step(h, inp):
        yi, c_c, b_c, x_c, gt = inp
        y = yi + jnp.einsum('bhis,bhsp->bhip', c_c, h)
        h = gt[..., None, None] * h + jnp.einsum('bhjs,bhjp->bhsp', b_c, x_c)
        return h, y

    h_fin, y = lax.scan(step, h0, (y_intra, c_dec, b_dec, xdt, g_tot))
    return from_chunks(y, l), h_fin


def block_diag_linear(x, w, b):
    xb = x.reshape(x.shape[:-1] + (C_BLOCKS, C_BLOCK))
    return jnp.einsum('blnc,ncd->blnd', xb, w).reshape(x.shape) + b


def rglru(x, r_logit, i_logit, lam, h0):
    log_a = -LRU_C * jax.nn.sigmoid(r_logit) * jax.nn.softplus(-lam)
    a = jnp.exp(log_a)
    b = jnp.sqrt(-jnp.expm1(2.0 * log_a)) * (jax.nn.sigmoid(i_logit) * x)
    b = b.at[:, 0].add(a[:, 0] * h0)

    def combine(left, right):
        a_l, b_l = left
        a_r, b_r = right
        return a_l * a_r, a_r * b_l + b_r

    _, h = lax.associative_scan(combine, (a, b), axis=1)
    return h, h[:, -1]


def split_cols(proj):
    sizes = (A_CONV_CH, A_WIDTH, A_HEADS, A_HEADS, B_WIDTH, B_CONV_CH, B_HEADS, C_WIDTH, C_WIDTH)
    idx = np.cumsum(np.array(sizes))[:-1].tolist()
    return jnp.split(proj, idx, axis=-1)


def mixer(h, st, lp):
    delta_s, delta_conv, ssd_h, ssd_conv, lru_h, lru_conv = st
    f32 = jnp.float32
    bsz, l, _ = h.shape
    proj = h @ lp['w_in']
    qkv_raw, z_a, b_a, a_a, z_b, xbc_raw, dt_raw, gate_c, x_c_raw = split_cols(proj)

    qkv, delta_conv_new = causal_conv(qkv_raw, delta_conv, lp['conv_a_w'])
    qkv = jax.nn.silu(qkv.astype(f32))
    q, k, v = jnp.split(qkv, [A_HEADS * A_DK, 2 * A_HEADS * A_DK], axis=-1)
    q = l2norm(q.reshape(bsz, l, A_HEADS, A_DK)) * (A_DK ** -0.5)
    k = l2norm(k.reshape(bsz, l, A_HEADS, A_DK))
    v = v.reshape(bsz, l, A_HEADS, A_DV)
    beta = jax.nn.sigmoid(b_a.astype(f32))
    log_alpha = -jnp.exp(lp['a_log_a'].astype(f32)) * jax.nn.softplus(a_a.astype(f32) + lp['dt_bias_a'])
    o_a, delta_s_new = gated_delta_rule(q, k, v, log_alpha, beta, delta_s.astype(f32))
    o_a = rmsnorm(o_a, lp['norm_a_w']) * jax.nn.silu(z_a.astype(f32).reshape(bsz, l, A_HEADS, A_DV))
    o_a = o_a.reshape(bsz, l, A_WIDTH)

    xbc, ssd_conv_new = causal_conv(xbc_raw, ssd_conv, lp['conv_b_w'])
    xbc = jax.nn.silu((xbc + lp['conv_b_b']).astype(f32))
    xs, bm, cm = jnp.split(xbc, [B_WIDTH, B_WIDTH + B_GROUPS * B_STATE], axis=-1)
    xs = xs.reshape(bsz, l, B_HEADS, B_HEADDIM)
    rep = B_HEADS // B_GROUPS
    bm = jnp.repeat(bm.reshape(bsz, l, B_GROUPS, B_STATE), rep, axis=2)
    cm = jnp.repeat(cm.reshape(bsz, l, B_GROUPS, B_STATE), rep, axis=2)
    dt = jax.nn.softplus(dt_raw.astype(f32) + lp['dt_bias_b'])
    y_b, ssd_h_new = ssd_scan(xs, dt, -jnp.exp(lp['a_log_b'].astype(f32)), bm, cm, ssd_h.astype(f32))
    y_b = y_b + lp['d_skip_b'][:, None] * xs
    y_b = y_b * jax.nn.silu(z_b.astype(f32).reshape(bsz, l, B_HEADS, B_HEADDIM))
    y_b = y_b.reshape(bsz, l, B_GROUPS, B_WIDTH // B_GROUPS)
    o_b = rmsnorm(y_b, lp['norm_b_w'].reshape(B_GROUPS, B_WIDTH // B_GROUPS)).reshape(bsz, l, B_WIDTH)

    xc, lru_conv_new = causal_conv(x_c_raw, lru_conv, lp['conv_c_w'])
    xc = (xc + lp['conv_c_b']).astype(f32)
    r_logit = block_diag_linear(xc, lp['w_rgate'], lp['b_rgate'])
    i_logit = block_diag_linear(xc, lp['w_igate'], lp['b_igate'])
    hc, lru_h_new = rglru(xc, r_logit, i_logit, lp['lru_lambda'].astype(f32), lru_h.astype(f32))
    o_c = hc * jax.nn.gelu(gate_c.astype(f32))

    mix = jnp.concatenate([o_a, o_b, o_c], axis=-1).astype(h.dtype) @ lp['w_out']
    return mix, (delta_s_new, delta_conv_new, ssd_h_new, ssd_conv_new, lru_h_new, lru_conv_new)


def run_trunk(x, states, params, norm_final):
    new_states = [[], [], [], [], [], []]
    for layer in range(DEPTH):
        lp = {name: arr[layer] for name, arr in params.items()}
        st = tuple(s[layer] for s in states)
        x = x + 0.5 * swiglu(rmsnorm(x, lp['norm_ffn1']), lp['ffn1_w_gate'], lp['ffn1_w_up'], lp['ffn1_w_down'])
        m, st_new = mixer(rmsnorm(x, lp['norm_mix']), st, lp)
        x = x + m
        x = x + 0.5 * swiglu(rmsnorm(x, lp['norm_ffn2']), lp['ffn2_w_gate'], lp['ffn2_w_up'], lp['ffn2_w_down'])
        for lst, s in zip(new_states, st_new):
            lst.append(s)
    return rmsnorm(x, norm_final), tuple(jnp.stack(lst) for lst in new_states)


def setup_inputs(seed: int = 0) -> dict:
    key = jax.random.key(seed)
    ks = iter(jax.random.split(key, 48))
    f32 = jnp.float32

    def nrm(shape, scale):
        return jax.random.normal(next(ks), shape, f32) * scale

    def unif(shape, lo, hi):
        return jax.random.uniform(next(ks), shape, f32, lo, hi)

    def gain(shape):
        return 1.0 + nrm(shape, 0.01)

    def dt_bias(shape):
        dt = jnp.exp(unif(shape, math.log(1e-3), math.log(1e-1)))
        return dt + jnp.log(-jnp.expm1(-dt))

    def lru_lambda(shape):
        p = unif(shape, 0.9, 0.999) ** (1.0 / LRU_C)
        return jnp.log(p) - jnp.log1p(-p)

    return {
        'x_prompt': nrm((BATCH, SEQ, D_MODEL), 1.0),
        'x_sample': nrm((DEC_BATCH, DEC_SEQ, D_MODEL), 1.0),
        'state_delta_s': nrm((DEPTH, DEC_BATCH, A_HEADS, A_DK, A_DV), 0.1),
        'state_delta_conv': nrm((DEPTH, DEC_BATCH, CONV_W - 1, A_CONV_CH), 1.0),
        'state_ssd_h': nrm((DEPTH, DEC_BATCH, B_HEADS, B_STATE, B_HEADDIM), 0.1),
        'state_ssd_conv': nrm((DEPTH, DEC_BATCH, CONV_W - 1, B_CONV_CH), 1.0),
        'state_lru_h': nrm((DEPTH, DEC_BATCH, C_WIDTH), 0.5),
        'state_lru_conv': nrm((DEPTH, DEC_BATCH, CONV_W - 1, C_WIDTH), 1.0),
        'norm_ffn1': gain((DEPTH, D_MODEL)),
        'ffn1_w_gate': nrm((DEPTH, D_MODEL, D_FF), D_MODEL ** -0.5),
        'ffn1_w_up': nrm((DEPTH, D_MODEL, D_FF), D_MODEL ** -0.5),
        'ffn1_w_down': nrm((DEPTH, D_FF, D_MODEL), D_FF ** -0.5),
        'norm_mix': gain((DEPTH, D_MODEL)),
        'w_in': nrm((DEPTH, D_MODEL, D_IN), D_MODEL ** -0.5),
        'conv_a_w': nrm((DEPTH, CONV_W, A_CONV_CH), 0.5),
        'a_log_a': jnp.log(unif((DEPTH, A_HEADS), 1.0, 16.0)),
        'dt_bias_a': dt_bias((DEPTH, A_HEADS)),
        'norm_a_w': gain((DEPTH, A_DV)),
        'conv_b_w': nrm((DEPTH, CONV_W, B_CONV_CH), 0.5),
        'conv_b_b': nrm((DEPTH, B_CONV_CH), 0.01),
        'a_log_b': jnp.log(unif((DEPTH, B_HEADS), 1.0, 16.0)),
        'dt_bias_b': dt_bias((DEPTH, B_HEADS)),
        'd_skip_b': 1.0 + nrm((DEPTH, B_HEADS), 0.1),
        'norm_b_w': gain((DEPTH, B_WIDTH)),
        'conv_c_w': nrm((DEPTH, CONV_W, C_WIDTH), 0.5),
        'conv_c_b': nrm((DEPTH, C_WIDTH), 0.01),
        'w_rgate': nrm((DEPTH, C_BLOCKS, C_BLOCK, C_BLOCK), C_BLOCK ** -0.5),
        'b_rgate': nrm((DEPTH, C_WIDTH), 0.01),
        'w_igate': nrm((DEPTH, C_BLOCKS, C_BLOCK, C_BLOCK), C_BLOCK ** -0.5),
        'b_igate': nrm((DEPTH, C_WIDTH), 0.01),
        'lru_lambda': lru_lambda((DEPTH, C_WIDTH)),
        'w_out': nrm((DEPTH, D_MIX, D_MODEL), D_MIX ** -0.5),
        'norm_ffn2': gain((DEPTH, D_MODEL)),
        'ffn2_w_gate': nrm((DEPTH, D_MODEL, D_FF), D_MODEL ** -0.5),
        'ffn2_w_up': nrm((DEPTH, D_MODEL, D_FF), D_MODEL ** -0.5),
        'ffn2_w_down': nrm((DEPTH, D_FF, D_MODEL), D_FF ** -0.5),
        'norm_final': gain((D_MODEL,)),
    }


def reference(x_prompt, x_sample, state_delta_s, state_delta_conv, state_ssd_h, state_ssd_conv,
              state_lru_h, state_lru_conv, norm_ffn1, ffn1_w_gate, ffn1_w_up, ffn1_w_down, norm_mix,
              w_in, conv_a_w, a_log_a, dt_bias_a, norm_a_w, conv_b_w, conv_b_b, a_log_b, dt_bias_b,
              d_skip_b, norm_b_w, conv_c_w, conv_c_b, w_rgate, b_rgate, w_igate, b_igate, lru_lambda,
              w_out, norm_ffn2, ffn2_w_gate, ffn2_w_up, ffn2_w_down, norm_final):
    params = {
        'norm_ffn1': norm_ffn1, 'ffn1_w_gate': ffn1_w_gate, 'ffn1_w_up': ffn1_w_up, 'ffn1_w_down': ffn1_w_down,
        'norm_mix': norm_mix, 'w_in': w_in,
        'conv_a_w': conv_a_w, 'a_log_a': a_log_a, 'dt_bias_a': dt_bias_a, 'norm_a_w': norm_a_w,
        'conv_b_w': conv_b_w, 'conv_b_b': conv_b_b, 'a_log_b': a_log_b, 'dt_bias_b': dt_bias_b,
        'd_skip_b': d_skip_b, 'norm_b_w': norm_b_w,
        'conv_c_w': conv_c_w, 'conv_c_b': conv_c_b, 'w_rgate': w_rgate, 'b_rgate': b_rgate,
        'w_igate': w_igate, 'b_igate': b_igate, 'lru_lambda': lru_lambda,
        'w_out': w_out,
        'norm_ffn2': norm_ffn2, 'ffn2_w_gate': ffn2_w_gate, 'ffn2_w_up': ffn2_w_up, 'ffn2_w_down': ffn2_w_down,
    }
    f32 = jnp.float32
    bp = x_prompt.shape[0]
    zero_states = (
        jnp.zeros((DEPTH, bp, A_HEADS, A_DK, A_DV), f32),
        jnp.zeros((DEPTH, bp, CONV_W - 1, A_CONV_CH), x_prompt.dtype),
        jnp.zeros((DEPTH, bp, B_HEADS, B_STATE, B_HEADDIM), f32),
        jnp.zeros((DEPTH, bp, CONV_W - 1, B_CONV_CH), x_prompt.dtype),
        jnp.zeros((DEPTH, bp, C_WIDTH), f32),
        jnp.zeros((DEPTH, bp, CONV_W - 1, C_WIDTH), x_prompt.dtype),
    )
    y_prompt, p_states = run_trunk(x_prompt, zero_states, params, norm_final)
    p_delta_s, p_delta_conv, p_ssd_h, p_ssd_conv, p_lru_h, p_lru_conv = p_states
    sample_states = (state_delta_s, state_delta_conv, state_ssd_h, state_ssd_conv, state_lru_h, state_lru_conv)
    y_sample, s_states = run_trunk(x_sample, sample_states, params, norm_final)
    s_delta_s, s_delta_conv, s_ssd_h, s_ssd_conv, s_lru_h, s_lru_conv = s_states
    return (y_prompt, y_sample, p_delta_s, p_delta_conv, p_ssd_h, p_ssd_conv, p_lru_h, p_lru_conv,
            s_delta_s, s_delta_conv, s_ssd_h, s_ssd_conv, s_lru_h, s_lru_conv)
```

```python
import functools

import jax
import jax.numpy as jnp
from jax import lax
from jax.experimental import pallas as pl
from jax.experimental.pallas import tpu as pltpu

F32 = jnp.float32
BF16 = jnp.bfloat16

D_MODEL = 1024
DEPTH = 2
D_FF = 2816
EPS = 1e-6
CONV_W = 4
A_HEADS = 6
A_DK = 64
A_DV = 64
A_WIDTH = A_HEADS * A_DV
A_CONV_CH = 2 * A_HEADS * A_DK + A_WIDTH
B_HEADS = 6
B_HEADDIM = 64
B_WIDTH = B_HEADS * B_HEADDIM
B_GROUPS = 2
B_STATE = 64
B_CONV_CH = B_WIDTH + 2 * B_GROUPS * B_STATE
C_WIDTH = 256
C_BLOCKS = 8
C_BLOCK = C_WIDTH // C_BLOCKS
LRU_C = 8.0
D_MIX = A_WIDTH + B_WIDTH + C_WIDTH

LANE = 128
SUBLANE = 8
HEAD = 64
N_PAIR = A_HEADS // 2
INV_BASE = 16

CONV_CH = A_CONV_CH + B_CONV_CH + C_WIDTH
XBC_OFF = A_CONV_CH
XC_OFF = A_CONV_CH + B_CONV_CH
GATE_CH = A_WIDTH + B_WIDTH + C_WIDTH + LANE
ZA_OFF, ZB_OFF, GC_OFF, SM_OFF = 0, A_WIDTH, A_WIDTH + B_WIDTH, A_WIDTH + B_WIDTH + C_WIDTH
D_IN_R = CONV_CH + GATE_CH
BETA_L, ALPHA_L, DT_L = 0, A_HEADS, 2 * A_HEADS
PV_ROWS = 8
PV_NORM, PV_CW, PV_CB, PV_MISC = 0, 1, 5, 6
M_BIAS, M_ALOG, M_NA, M_DSKIP, M_NB, M_LAM, M_BRI = 0, 128, 256, 384, 768, 1152, 1408

VMEM_LIMIT = 56 * 1024 * 1024


def _dot(a, b):
    return jnp.dot(a, b, preferred_element_type=F32)


def _dot_nt(a, b):
    return lax.dot_general(a, b, (((1,), (1,)), ((), ())), preferred_element_type=F32)


def _dot_tn(a, b):
    return lax.dot_general(a, b, (((0,), (0,)), ((), ())), preferred_element_type=F32)


def _bdot(a, b):
    return _dot(a.astype(BF16), b.astype(BF16))


def _split(x, n):
    parts = []
    r = x
    for _ in range(n):
        p = r.astype(BF16)
        parts.append(p)
        r = r - p.astype(F32)
    return parts


def _dot_left_exact(m, x, n):
    return sum(_dot(m, p) for p in _split(x, n))


def _dot_right_exact(x, m, n):
    return sum(_dot(p, m) for p in _split(x, n))


def _sigmoid(x):
    return jax.nn.sigmoid(x)


def _silu(x):
    return x * _sigmoid(x)


def _softplus(x):
    return jnp.maximum(x, 0.0) + jnp.log1p(jnp.exp(-jnp.abs(x)))


def _gelu_tanh(x):
    return 0.5 * x * (1.0 + jnp.tanh(0.7978845608028654 * (x + 0.044715 * x * x * x)))


def _rms(x, w):
    ms = jnp.mean(x * x, axis=-1, keepdims=True)
    return x * lax.rsqrt(ms + EPS) * w


def _ffn_body(x_ref, nw_ref, wg_ref, wu_ref, wd_ref, fnw_ref, o_ref, n_split):
    x = x_ref[...]
    h = _rms(x, nw_ref[...]).astype(BF16)
    fs = D_FF // n_split
    acc = None
    for j in range(n_split):
        g = _dot(h, wg_ref[:, j * fs:(j + 1) * fs])
        u = _dot(h, wu_ref[:, j * fs:(j + 1) * fs])
        a = (_silu(g) * u).astype(BF16)
        d = _dot(a, wd_ref[j * fs:(j + 1) * fs, :])
        acc = d if acc is None else acc + d
    y = x + 0.5 * acc
    if fnw_ref is not None:
        y = _rms(y, fnw_ref[...])
    o_ref[...] = y


def _ffn_kernel(x_ref, nw_ref, wg_ref, wu_ref, wd_ref, o_ref, *, n_split):
    _ffn_body(x_ref, nw_ref, wg_ref, wu_ref, wd_ref, None, o_ref, n_split)


def _ffn_final_kernel(x_ref, nw_ref, wg_ref, wu_ref, wd_ref, fnw_ref, o_ref, *, n_split):
    _ffn_body(x_ref, nw_ref, wg_ref, wu_ref, wd_ref, fnw_ref, o_ref, n_split)


def _resident(shape):
    return pl.BlockSpec(shape, lambda *_: (0,) * len(shape), pipeline_mode=pl.Buffered(1))


def _ffn(x2d, nw, wg, wu, wd, final_w=None, *, tm):
    t = x2d.shape[0]
    assert t % tm == 0
    row = pl.BlockSpec((tm, D_MODEL), lambda i: (i, 0))
    in_specs = [row, _resident((1, D_MODEL)), _resident((D_MODEL, D_FF)), _resident((D_MODEL, D_FF)),
                _resident((D_FF, D_MODEL))]
    args = [x2d, nw, wg, wu, wd]
    if final_w is None:
        body = functools.partial(_ffn_kernel, n_split=2)
    else:
        body = functools.partial(_ffn_final_kernel, n_split=2)
        in_specs.append(_resident((1, D_MODEL)))
        args.append(final_w)
    return pl.pallas_call(
        body,
        out_shape=jax.ShapeDtypeStruct((t, D_MODEL), F32),
        grid=(t // tm,),
        in_specs=in_specs,
        out_specs=row,
        compiler_params=pltpu.CompilerParams(dimension_semantics=("arbitrary",), vmem_limit_bytes=VMEM_LIMIT),
        name="ffn_final" if final_w is not None else "ffn",
    )(*args)


def _tri_inv(lmat, c):
    ri = lax.broadcasted_iota(jnp.int32, (c, c), 0)
    ci = lax.broadcasted_iota(jnp.int32, (c, c), 1)
    ld = jnp.where(ri // INV_BASE == ci // INV_BASE, lmat, 0.0)
    t = jnp.where(ri == ci, 1.0, 0.0) - ld
    p = ld
    n = 2
    while n < INV_BASE:
        p = _bdot(p, p)
        t = t + _bdot(t, p)
        n *= 2
    s = INV_BASE
    while s < c:
        sub = (ri // (2 * s) == ci // (2 * s)) & ((ri // s) % 2 == 1) & ((ci // s) % 2 == 0)
        e = jnp.where(sub, lmat, 0.0)
        t = t - _bdot(t, _bdot(e, t))
        s *= 2
    return t


def _mixer_kernel(x_ref, pv_ref, win_ref, wri_ref, wout_ref, s0_ref, h0_ref, l0_ref, c0_ref,
                  xo_ref, s_ref, hs_ref, l_ref, co_ref,
                  cbuf, act, gates, mixo, *, chunk, tb):
    c = chunk
    t_idx = pl.program_id(1)

    @pl.when(t_idx == 0)
    def _():
        s_ref[...] = s0_ref[...]
        hs_ref[...] = h0_ref[...]
        l_ref[...] = l0_ref[...]
        cbuf[0:SUBLANE, :] = c0_ref[0]

    hb = _rms(x_ref[0], pv_ref[PV_NORM:PV_NORM + 1, 0:D_MODEL]).astype(BF16)
    cbuf[SUBLANE:SUBLANE + tb, :] = _dot(hb, win_ref[:, 0:CONV_CH])
    gates[...] = _dot(hb, win_ref[:, CONV_CH:D_IN_R])

    first = SUBLANE - (CONV_W - 1)
    for j in range(CONV_CH // LANE):
        cs = slice(j * LANE, (j + 1) * LANE)
        acc = cbuf[first:first + tb, cs] * pv_ref[PV_CW:PV_CW + 1, cs]
        for k in range(1, CONV_W):
            acc = acc + cbuf[first + k:first + k + tb, cs] * pv_ref[PV_CW + k:PV_CW + k + 1, cs]
        if j * LANE >= XBC_OFF:
            acc = acc + pv_ref[PV_CB:PV_CB + 1, cs]
        if j * LANE < XC_OFF:
            acc = _silu(acc)
        act[:, cs] = acc
    tail = cbuf[tb:tb + SUBLANE, :]
    co_ref[0] = tail
    cbuf[0:SUBLANE, :] = tail

    lane1 = lax.broadcasted_iota(jnp.int32, (1, LANE), 1)
    lo = lane1 < HEAD
    sm = gates[:, SM_OFF:SM_OFF + LANE]
    misc = pv_ref[PV_MISC:PV_MISC + 1, :]
    dec_lanes = (lane1 >= ALPHA_L) & (lane1 < DT_L + B_HEADS)
    nega = jnp.where(dec_lanes, -jnp.exp(misc[:, M_ALOG:M_ALOG + LANE]), 0.0)
    beta = _sigmoid(sm)
    sp = _softplus(sm + misc[:, M_BIAS:M_BIAS + LANE])
    la = sp * nega

    ri = lax.broadcasted_iota(jnp.int32, (c, c), 0)
    ci = lax.broadcasted_iota(jnp.int32, (c, c), 1)
    causal = ri >= ci
    strict = ri > ci
    tril = jnp.where(causal, 1.0, 0.0).astype(BF16)
    r128 = lax.broadcasted_iota(jnp.int32, (LANE, LANE), 0)
    c128 = lax.broadcasted_iota(jnp.int32, (LANE, LANE), 1)
    same_head = (r128 // HEAD) == (c128 // HEAD)
    ones_head = jnp.where(same_head, 1.0, 0.0).astype(BF16)
    row_lo = r128 < HEAD
    g3r = lax.broadcasted_iota(jnp.int32, (B_WIDTH, B_WIDTH), 0) // (B_WIDTH // B_GROUPS)
    g3c = lax.broadcasted_iota(jnp.int32, (B_WIDTH, B_WIDTH), 1) // (B_WIDTH // B_GROUPS)
    ones_group = jnp.where(g3r == g3c, 1.0, 0.0).astype(BF16)

    def pairvec(arr, la_, lb_):
        return jnp.where(lo, arr[:, la_:la_ + 1], arr[:, lb_:lb_ + 1])

    def head_sum(v):
        return _dot_right_exact(v, ones_head, 2)

    for ck in range(tb // c):
        rs = slice(ck * c, (ck + 1) * c)
        g = _dot_left_exact(tril, la[rs], 3)
        gt_ = g.T
        eg = jnp.exp(g)
        glast = g[c - 1:c, :]
        egl = jnp.exp(glast - g)
        gtot = jnp.exp(glast)
        beta_c = beta[rs]
        sp_c = sp[rs]

        def decay(lane_idx):
            d = g[:, lane_idx:lane_idx + 1] - gt_[lane_idx:lane_idx + 1, :]
            return jnp.exp(jnp.where(causal, d, -jnp.inf))

        def state_scale(la_, lb_):
            return jnp.where(row_lo, gtot[:, la_:la_ + 1], gtot[:, lb_:lb_ + 1])

        for p in range(N_PAIR):
            ha, hb_ = 2 * p, 2 * p + 1
            ps = slice(p * LANE, (p + 1) * LANE)
            qp = act[rs, ps]
            kp = act[rs, slice(A_HEADS * A_DK + p * LANE, A_HEADS * A_DK + (p + 1) * LANE)]
            vp = act[rs, slice(2 * A_HEADS * A_DK + p * LANE, 2 * A_HEADS * A_DK + (p + 1) * LANE)]
            qn = qp * lax.rsqrt(head_sum(qp * qp) + EPS) * (A_DK ** -0.5)
            kn = kp * lax.rsqrt(head_sum(kp * kp) + EPS)
            bvec = pairvec(beta_c, BETA_L + ha, BETA_L + hb_)
            egv = pairvec(eg, ALPHA_L + ha, ALPHA_L + hb_)
            eglv = pairvec(egl, ALPHA_L + ha, ALPHA_L + hb_)
            rhs = jnp.concatenate([bvec * vp, bvec * egv * kn], axis=1).astype(BF16)
            knb = kn.astype(BF16)
            uw, qk = [], []
            for h, sel in ((ha, lo), (hb_, jnp.logical_not(lo))):
                kh = jnp.where(sel, kn, 0.0).astype(BF16)
                qh = jnp.where(sel, qn, 0.0).astype(BF16)
                gam = decay(ALPHA_L + h)
                kk = _dot_nt(kh, knb)
                lmat = jnp.where(strict, beta_c[:, BETA_L + h:BETA_L + h + 1] * kk * gam, 0.0)
                tinv = _tri_inv(lmat, c)
                uw.append(_dot(tinv.astype(BF16), rhs))
                qk.append((_dot_nt(qh, knb) * gam).astype(BF16))
            u = jnp.where(lo, uw[0][:, :LANE], uw[1][:, :LANE])
            w = jnp.where(lo, uw[0][:, LANE:], uw[1][:, LANE:])
            qd = qn * egv
            kd = kn * eglv
            s_old = s_ref[0, p]
            wq = _bdot(jnp.concatenate([w, qd], axis=0), s_old)
            delta = u - wq[:c]
            db = delta.astype(BF16)
            o = wq[c:] + jnp.where(lo, _dot(qk[0], db), _dot(qk[1], db))
            upd = _dot_tn(kd.astype(BF16), db)
            s_ref[0, p] = state_scale(ALPHA_L + ha, ALPHA_L + hb_) * s_old + jnp.where(same_head, upd, 0.0)
            ms = head_sum(o * o) * (1.0 / A_DV)
            on = o * lax.rsqrt(ms + EPS) * misc[:, M_NA:M_NA + LANE]
            za = gates[rs, slice(ZA_OFF + p * LANE, ZA_OFF + (p + 1) * LANE)]
            mixo[rs, ps] = (on * _silu(za)).astype(BF16)

        bm = act[rs, slice(XBC_OFF + B_WIDTH, XBC_OFF + B_WIDTH + LANE)]
        cm = act[rs, slice(XBC_OFF + B_WIDTH + LANE, XBC_OFF + B_WIDTH + 2 * LANE)]
        bm_sw = pltpu.roll(bm, HEAD, 1)
        cm_sw = pltpu.roll(cm, HEAD, 1)
        bmb = bm.astype(BF16)
        scores = [_dot_nt(jnp.where(lo, cm, 0.0).astype(BF16), bmb),
                  _dot_nt(jnp.where(lo, 0.0, cm).astype(BF16), bmb)]
        ys = []
        for p in range(N_PAIR):
            ha, hb_ = 2 * p, 2 * p + 1
            ga, gb = ha // (B_HEADS // B_GROUPS), hb_ // (B_HEADS // B_GROUPS)
            xs = act[rs, slice(XBC_OFF + p * LANE, XBC_OFF + (p + 1) * LANE)]
            dtv = pairvec(sp_c, DT_L + ha, DT_L + hb_)
            xdt = xs * dtv
            xb = xdt.astype(BF16)
            ma = (scores[ga] * decay(DT_L + ha)).astype(BF16)
            mb = (scores[gb] * decay(DT_L + hb_)).astype(BF16)
            y = jnp.where(lo, _dot(ma, xb), _dot(mb, xb))
            if ga == gb == 0:
                bsel, csel = jnp.where(lo, bm, bm_sw), jnp.where(lo, cm, cm_sw)
            elif ga == gb == 1:
                bsel, csel = jnp.where(lo, bm_sw, bm), jnp.where(lo, cm_sw, cm)
            else:
                bsel, csel = bm, cm
            egv = pairvec(eg, DT_L + ha, DT_L + hb_)
            eglv = pairvec(egl, DT_L + ha, DT_L + hb_)
            h_old = hs_ref[0, p]
            y = y + _bdot(csel * egv, h_old)
            upd = _dot_tn((bsel * eglv).astype(BF16), xb)
            hs_ref[0, p] = state_scale(DT_L + ha, DT_L + hb_) * h_old + jnp.where(same_head, upd, 0.0)
            y = y + misc[:, M_DSKIP + p * LANE:M_DSKIP + (p + 1) * LANE] * xs
            zb = gates[rs, slice(ZB_OFF + p * LANE, ZB_OFF + (p + 1) * LANE)]
            ys.append(y * _silu(zb))
        yb = jnp.concatenate(ys, axis=1)
        ms = _dot_right_exact(yb * yb, ones_group, 2) * (1.0 / (B_WIDTH // B_GROUPS))
        ob = yb * lax.rsqrt(ms + EPS) * misc[:, M_NB:M_NB + B_WIDTH]
        mixo[rs, A_WIDTH:A_WIDTH + B_WIDTH] = ob.astype(BF16)

    xc = act[:, XC_OFF:XC_OFF + C_WIDTH]
    rig = _dot(xc.astype(BF16), wri_ref[...]) + misc[:, M_BRI:M_BRI + 2 * C_WIDTH]
    rowt = lax.broadcasted_iota(jnp.int32, (tb, LANE), 0)
    for half in range(C_WIDTH // LANE):
        hs_ = slice(half * LANE, (half + 1) * LANE)
        xch = xc[:, hs_]
        lam = misc[:, M_LAM + half * LANE:M_LAM + (half + 1) * LANE]
        log_a = -LRU_C * _sigmoid(rig[:, hs_]) * _softplus(-lam)
        a = jnp.exp(log_a)
        b = jnp.sqrt(1.0 - jnp.exp(2.0 * log_a)) * (_sigmoid(rig[:, C_WIDTH + half * LANE:C_WIDTH + (half + 1) * LANE]) * xch)
        d = 1
        while d < tb:
            keep = rowt >= d
            a_sh = jnp.where(keep, pltpu.roll(a, d, 0), 1.0)
            b_sh = jnp.where(keep, pltpu.roll(b, d, 0), 0.0)
            b = a * b_sh + b
            a = a * a_sh
            d *= 2
        hseq = a * l_ref[0, :, hs_] + b
        l_ref[0, :, hs_] = hseq[tb - 1:tb, :]
        gc = gates[:, GC_OFF + half * LANE:GC_OFF + (half + 1) * LANE]
        mixo[:, A_WIDTH + B_WIDTH + half * LANE:A_WIDTH + B_WIDTH + (half + 1) * LANE] = (hseq * _gelu_tanh(gc)).astype(BF16)

    xo_ref[0] = x_ref[0] + _dot(mixo[...], wout_ref[...])


def _mixer(x, pv, win, wri, wout, s0, h0, l0, c0, *, chunk, tb):
    bsz, l, _ = x.shape
    assert l % tb == 0 and tb % chunk == 0
    seq = lambda b, t: (b, 0, 0)
    seq4 = lambda b, t: (b, 0, 0, 0)
    in_specs = [
        pl.BlockSpec((1, tb, D_MODEL), lambda b, t: (b, t, 0)),
        _resident((PV_ROWS, CONV_CH)),
        _resident((D_MODEL, D_IN_R)),
        _resident((C_WIDTH, 2 * C_WIDTH)),
        _resident((D_MIX, D_MODEL)),
        pl.BlockSpec((1, N_PAIR, LANE, LANE), seq4),
        pl.BlockSpec((1, N_PAIR, LANE, LANE), seq4),
        pl.BlockSpec((1, 1, C_WIDTH), seq),
        pl.BlockSpec((1, SUBLANE, CONV_CH), seq),
    ]
    out_specs = [
        pl.BlockSpec((1, tb, D_MODEL), lambda b, t: (b, t, 0)),
        pl.BlockSpec((1, N_PAIR, LANE, LANE), seq4),
        pl.BlockSpec((1, N_PAIR, LANE, LANE), seq4),
        pl.BlockSpec((1, 1, C_WIDTH), seq),
        pl.BlockSpec((1, SUBLANE, CONV_CH), seq),
    ]
    out_shape = [
        jax.ShapeDtypeStruct((bsz, l, D_MODEL), F32),
        jax.ShapeDtypeStruct((bsz, N_PAIR, LANE, LANE), F32),
        jax.ShapeDtypeStruct((bsz, N_PAIR, LANE, LANE), F32),
        jax.ShapeDtypeStruct((bsz, 1, C_WIDTH), F32),
        jax.ShapeDtypeStruct((bsz, SUBLANE, CONV_CH), F32),
    ]
    scratch = [
        pltpu.VMEM((tb + SUBLANE, CONV_CH), F32),
        pltpu.VMEM((tb, CONV_CH), F32),
        pltpu.VMEM((tb, GATE_CH), F32),
        pltpu.VMEM((tb, D_MIX), BF16),
    ]
    return pl.pallas_call(
        functools.partial(_mixer_kernel, chunk=chunk, tb=tb),
        out_shape=out_shape,
        grid=(bsz, l // tb),
        in_specs=in_specs,
        out_specs=out_specs,
        scratch_shapes=scratch,
        compiler_params=pltpu.CompilerParams(dimension_semantics=("arbitrary", "arbitrary"),
                                             vmem_limit_bytes=VMEM_LIMIT),
        name="mixer_c%d" % chunk,
    )(x, pv, win, wri, wout, s0, h0, l0, c0)


def _pair_states(s):
    bsz = s.shape[0]
    s = s.reshape(bsz, N_PAIR, 2, HEAD, HEAD)
    z = jnp.zeros((bsz, N_PAIR, HEAD, HEAD), s.dtype)
    top = jnp.concatenate([s[:, :, 0], z], axis=-1)
    bot = jnp.concatenate([z, s[:, :, 1]], axis=-1)
    return jnp.concatenate([top, bot], axis=-2)


def _unpair_states(sp):
    a = sp[:, :, :HEAD, :HEAD]
    b = sp[:, :, HEAD:, HEAD:]
    return jnp.stack([a, b], axis=2).reshape(sp.shape[0], A_HEADS, HEAD, HEAD)


def _layer_params(lp):
    w = lp['w_in']
    o = 0
    cols = {}
    for name, size in (('qkv', A_CONV_CH), ('z_a', A_WIDTH), ('b_a', A_HEADS), ('a_a', A_HEADS), ('z_b', B_WIDTH),
                       ('xbc', B_CONV_CH), ('dt', B_HEADS), ('gate_c', C_WIDTH), ('x_c', C_WIDTH)):
        cols[name] = w[:, o:o + size]
        o += size
    pad = jnp.zeros((D_MODEL, LANE - 2 * A_HEADS - B_HEADS), w.dtype)
    win = jnp.concatenate([cols['qkv'], cols['xbc'], cols['x_c'], cols['z_a'], cols['z_b'], cols['gate_c'],
                           cols['b_a'], cols['a_a'], cols['dt'], pad], axis=1).astype(BF16)

    def lane_block(*pieces):
        v = jnp.zeros((LANE,), F32)
        for off, val in pieces:
            v = lax.dynamic_update_slice(v, val.astype(F32), (off,))
        return v

    misc = jnp.concatenate([
        lane_block((ALPHA_L, lp['dt_bias_a']), (DT_L, lp['dt_bias_b'])),
        lane_block((ALPHA_L, lp['a_log_a']), (DT_L, lp['a_log_b'])),
        jnp.tile(lp['norm_a_w'], 2),
        jnp.repeat(lp['d_skip_b'], B_HEADDIM),
        lp['norm_b_w'],
        lp['lru_lambda'],
        lp['b_rgate'], lp['b_igate'],
    ])
    misc = jnp.pad(misc, (0, CONV_CH - misc.shape[0]))
    cw = jnp.concatenate([lp['conv_a_w'], lp['conv_b_w'], lp['conv_c_w']], axis=1)
    cb = jnp.concatenate([jnp.zeros((A_CONV_CH,), F32), lp['conv_b_b'], lp['conv_c_b']])
    pv = jnp.concatenate([jnp.pad(lp['norm_mix'], (0, CONV_CH - D_MODEL))[None], cw, cb[None], misc[None],
                          jnp.zeros((1, CONV_CH), F32)], axis=0)

    def block_diag(wb):
        eye = jnp.eye(C_BLOCKS, dtype=wb.dtype)
        return jnp.einsum('ncd,nm->ncmd', wb, eye).reshape(C_WIDTH, C_WIDTH)

    wri = jnp.concatenate([block_diag(lp['w_rgate']), block_diag(lp['w_igate'])], axis=1).astype(BF16)
    return dict(
        pv=pv, win=win, wri=wri, wout=lp['w_out'].astype(BF16),
        n1=lp['norm_ffn1'][None], g1=lp['ffn1_w_gate'].astype(BF16), u1=lp['ffn1_w_up'].astype(BF16),
        d1=lp['ffn1_w_down'].astype(BF16),
        n2=lp['norm_ffn2'][None], g2=lp['ffn2_w_gate'].astype(BF16), u2=lp['ffn2_w_up'].astype(BF16),
        d2=lp['ffn2_w_down'].astype(BF16),
    )


def _trunk(x, states, layers, norm_final, *, chunk, tb, tm):
    bsz, l, _ = x.shape
    delta_s, delta_conv, ssd_h, ssd_conv, lru_h, lru_conv = states
    outs = [[] for _ in range(6)]
    for layer in range(DEPTH):
        lw = layers[layer]
        x2 = _ffn(x.reshape(bsz * l, D_MODEL), lw['n1'], lw['g1'], lw['u1'], lw['d1'], tm=tm)
        conv0 = jnp.concatenate([delta_conv[layer], ssd_conv[layer], lru_conv[layer]], axis=-1).astype(F32)
        conv0 = jnp.pad(conv0, ((0, 0), (SUBLANE - (CONV_W - 1), 0), (0, 0)))
        x3, s_new, h_new, l_new, c_new = _mixer(
            x2.reshape(bsz, l, D_MODEL), lw['pv'], lw['win'], lw['wri'], lw['wout'],
            _pair_states(delta_s[layer].astype(F32)), _pair_states(ssd_h[layer].astype(F32)),
            lru_h[layer].astype(F32)[:, None, :], conv0, chunk=chunk, tb=tb)
        fin = norm_final[None] if layer == DEPTH - 1 else None
        x = _ffn(x3.reshape(bsz * l, D_MODEL), lw['n2'], lw['g2'], lw['u2'], lw['d2'], fin, tm=tm).reshape(bsz, l, D_MODEL)
        c_new = c_new[:, SUBLANE - (CONV_W - 1):, :]
        outs[0].append(_unpair_states(s_new))
        outs[1].append(c_new[..., :A_CONV_CH])
        outs[2].append(_unpair_states(h_new))
        outs[3].append(c_new[..., XBC_OFF:XBC_OFF + B_CONV_CH])
        outs[4].append(l_new[:, 0, :])
        outs[5].append(c_new[..., XC_OFF:])
    return x, tuple(jnp.stack(o) for o in outs)


def kernel(x_prompt, x_sample, state_delta_s, state_delta_conv, state_ssd_h, state_ssd_conv, state_lru_h, state_lru_conv, norm_ffn1, ffn1_w_gate, ffn1_w_up, ffn1_w_down, norm_mix, w_in, conv_a_w, a_log_a, dt_bias_a, norm_a_w, conv_b_w, conv_b_b, a_log_b, dt_bias_b, d_skip_b, norm_b_w, conv_c_w, conv_c_b, w_rgate, b_rgate, w_igate, b_igate, lru_lambda, w_out, norm_ffn2, ffn2_w_gate, ffn2_w_up, ffn2_w_down, norm_final):
    params = {
        'norm_ffn1': norm_ffn1, 'ffn1_w_gate': ffn1_w_gate, 'ffn1_w_up': ffn1_w_up, 'ffn1_w_down': ffn1_w_down,
        'norm_mix': norm_mix, 'w_in': w_in,
        'conv_a_w': conv_a_w, 'a_log_a': a_log_a, 'dt_bias_a': dt_bias_a, 'norm_a_w': norm_a_w,
        'conv_b_w': conv_b_w, 'conv_b_b': conv_b_b, 'a_log_b': a_log_b, 'dt_bias_b': dt_bias_b,
        'd_skip_b': d_skip_b, 'norm_b_w': norm_b_w,
        'conv_c_w': conv_c_w, 'conv_c_b': conv_c_b, 'w_rgate': w_rgate, 'b_rgate': b_rgate,
        'w_igate': w_igate, 'b_igate': b_igate, 'lru_lambda': lru_lambda,
        'w_out': w_out,
        'norm_ffn2': norm_ffn2, 'ffn2_w_gate': ffn2_w_gate, 'ffn2_w_up': ffn2_w_up, 'ffn2_w_down': ffn2_w_down,
    }
    layers = [_layer_params({k: v[i] for k, v in params.items()}) for i in range(DEPTH)]
    bp = x_prompt.shape[0]
    zero_states = (
        jnp.zeros((DEPTH, bp, A_HEADS, A_DK, A_DV), F32),
        jnp.zeros((DEPTH, bp, CONV_W - 1, A_CONV_CH), F32),
        jnp.zeros((DEPTH, bp, B_HEADS, B_STATE, B_HEADDIM), F32),
        jnp.zeros((DEPTH, bp, CONV_W - 1, B_CONV_CH), F32),
        jnp.zeros((DEPTH, bp, C_WIDTH), F32),
        jnp.zeros((DEPTH, bp, CONV_W - 1, C_WIDTH), F32),
    )
    y_prompt, p_states = _trunk(x_prompt, zero_states, layers, norm_final, chunk=128, tb=256, tm=512)
    sample_states = (state_delta_s, state_delta_conv, state_ssd_h, state_ssd_conv, state_lru_h, state_lru_conv)
    y_sample, s_states = _trunk(x_sample, sample_states, layers, norm_final, chunk=64, tb=64, tm=512)
    return (y_prompt, y_sample) + tuple(p_states) + tuple(s_states)
```

```python
import functools

import jax
import jax.numpy as jnp
from jax import lax
from jax.experimental import pallas as pl
from jax.experimental.pallas import tpu as pltpu

F32 = jnp.float32
BF16 = jnp.bfloat16

D_MODEL = 1024
DEPTH = 2
D_FF = 2816
EPS = 1e-6
CONV_W = 4
A_HEADS = 6
A_DK = 64
A_DV = 64
A_WIDTH = A_HEADS * A_DV
A_CONV_CH = 2 * A_HEADS * A_DK + A_WIDTH
B_HEADS = 6
B_HEADDIM = 64
B_WIDTH = B_HEADS * B_HEADDIM
B_GROUPS = 2
B_STATE = 64
B_CONV_CH = B_WIDTH + 2 * B_GROUPS * B_STATE
C_WIDTH = 256
C_BLOCKS = 8
C_BLOCK = C_WIDTH // C_BLOCKS
LRU_C = 8.0
D_MIX = A_WIDTH + B_WIDTH + C_WIDTH

LANE = 128
SUBLANE = 8
HEAD = 64
N_PAIR = A_HEADS // 2
INV_BASE = 16

CONV_CH = A_CONV_CH + B_CONV_CH + C_WIDTH
K_OFF = A_HEADS * A_DK
V_OFF = 2 * A_HEADS * A_DK
XBC_OFF = A_CONV_CH
BM_OFF = XBC_OFF + B_WIDTH
CM_OFF = BM_OFF + B_GROUPS * B_STATE
XC_OFF = A_CONV_CH + B_CONV_CH
GATE_CH = A_WIDTH + B_WIDTH + C_WIDTH + LANE
ZA_OFF, ZB_OFF, GC_OFF, SM_OFF = 0, A_WIDTH, A_WIDTH + B_WIDTH, A_WIDTH + B_WIDTH + C_WIDTH
D_IN_R = CONV_CH + GATE_CH
BETA_L, ALPHA_L, DT_L = 0, A_HEADS, 2 * A_HEADS
PV_ROWS = 8
PV_NORM, PV_CW, PV_CB, PV_MISC = 0, 1, 5, 6
M_BIAS, M_ALOG, M_NA, M_DSKIP, M_NB, M_LAM, M_BRI = 0, 128, 256, 384, 768, 1152, 1408

VMEM_LIMIT = 56 * 1024 * 1024


def _dot(a, b):
    return jnp.dot(a, b, preferred_element_type=F32)


def _dot_nt(a, b):
    return lax.dot_general(a, b, (((1,), (1,)), ((), ())), preferred_element_type=F32)


def _dot_tn(a, b):
    return lax.dot_general(a, b, (((0,), (0,)), ((), ())), preferred_element_type=F32)


def _split(x, n):
    parts = []
    r = x
    for _ in range(n):
        p = r.astype(BF16)
        parts.append(p)
        r = r - p.astype(F32)
    return parts


def _dot_left_exact(m, x, n):
    return sum(_dot(m, p) for p in _split(x, n))


def _dot_right_exact(x, m, n):
    return sum(_dot(p, m) for p in _split(x, n))


def _sigmoid(x):
    return jax.nn.sigmoid(x)


def _silu(x):
    return x * _sigmoid(x)


def _softplus(x):
    return jnp.maximum(x, 0.0) + jnp.log1p(jnp.exp(-jnp.abs(x)))


def _gelu_tanh(x):
    return 0.5 * x * (1.0 + jnp.tanh(0.7978845608028654 * (x + 0.044715 * x * x * x)))


def _rms(x, w):
    ms = jnp.mean(x * x, axis=-1, keepdims=True)
    return x * lax.rsqrt(ms + EPS) * w


def _ffn_body(x_ref, nw_ref, wg_ref, wu_ref, wd_ref, fnw_ref, o_ref, n_split):
    x = x_ref[...]
    h = _rms(x, nw_ref[...]).astype(BF16)
    fs = D_FF // n_split
    acc = None
    for j in range(n_split):
        g = _dot(h, wg_ref[:, j * fs:(j + 1) * fs])
        u = _dot(h, wu_ref[:, j * fs:(j + 1) * fs])
        a = (_silu(g) * u).astype(BF16)
        d = _dot(a, wd_ref[j * fs:(j + 1) * fs, :])
        acc = d if acc is None else acc + d
    y = x + 0.5 * acc
    if fnw_ref is not None:
        y = _rms(y, fnw_ref[...])
    o_ref[...] = y


def _ffn_kernel(x_ref, nw_ref, wg_ref, wu_ref, wd_ref, o_ref, *, n_split):
    _ffn_body(x_ref, nw_ref, wg_ref, wu_ref, wd_ref, None, o_ref, n_split)


def _ffn_final_kernel(x_ref, nw_ref, wg_ref, wu_ref, wd_ref, fnw_ref, o_ref, *, n_split):
    _ffn_body(x_ref, nw_ref, wg_ref, wu_ref, wd_ref, fnw_ref, o_ref, n_split)


def _resident(shape):
    return pl.BlockSpec(shape, lambda *_: (0,) * len(shape), pipeline_mode=pl.Buffered(1))


def _ffn(x2d, nw, wg, wu, wd, final_w=None, *, tm):
    t = x2d.shape[0]
    assert t % tm == 0
    row = pl.BlockSpec((tm, D_MODEL), lambda i: (i, 0))
    in_specs = [row, _resident((1, D_MODEL)), _resident((D_MODEL, D_FF)), _resident((D_MODEL, D_FF)),
                _resident((D_FF, D_MODEL))]
    args = [x2d, nw, wg, wu, wd]
    if final_w is None:
        body = functools.partial(_ffn_kernel, n_split=2)
    else:
        body = functools.partial(_ffn_final_kernel, n_split=2)
        in_specs.append(_resident((1, D_MODEL)))
        args.append(final_w)
    return pl.pallas_call(
        body,
        out_shape=jax.ShapeDtypeStruct((t, D_MODEL), F32),
        grid=(t // tm,),
        in_specs=in_specs,
        out_specs=row,
        compiler_params=pltpu.CompilerParams(dimension_semantics=("arbitrary",), vmem_limit_bytes=VMEM_LIMIT),
        name="ffn_final" if final_w is not None else "ffn",
    )(*args)


def _wide_blockdiag(xw, c):
    if c % LANE == 0:
        z = jnp.zeros((c, c), BF16)
        xb = xw.astype(BF16)
        return jnp.concatenate([jnp.concatenate([xb[:, :c], z], axis=1),
                                jnp.concatenate([z, xb[:, c:]], axis=1)], axis=0)
    first = lax.broadcasted_iota(jnp.int32, (1, 2 * c), 1) < c
    return jnp.concatenate([jnp.where(first, xw, 0.0).astype(BF16), jnp.where(first, 0.0, xw).astype(BF16)], axis=0)


def _head_split_rows(x, lo):
    return jnp.concatenate([jnp.where(lo, x, 0.0), jnp.where(lo, 0.0, x)], axis=0)


def _tri_inv_wide(lws, c):
    ri = lax.broadcasted_iota(jnp.int32, (c, 2 * c), 0)
    cj = lax.broadcasted_iota(jnp.int32, (c, 2 * c), 1) % c
    diag_blk = ri // INV_BASE == cj // INV_BASE
    eye = jnp.where(ri == cj, 1.0, 0.0)

    def mm(a, bw):
        return _dot(a.astype(BF16), _wide_blockdiag(bw, c))

    ns = [jnp.where(diag_blk, -lw, 0.0) for lw in lws]
    ts = [eye + n for n in ns]
    ps = [mm(n, n) for n in ns]
    n_pow = 2
    while 2 * n_pow < INV_BASE:
        outs = [_dot(jnp.concatenate([p.astype(BF16), t.astype(BF16)], axis=0),
                     _wide_blockdiag(p, c)) for p, t in zip(ps, ts)]
        ts = [t + o[c:] for t, o in zip(ts, outs)]
        ps = [o[:c] for o in outs]
        n_pow *= 2
    ts = [t + mm(t, p) for t, p in zip(ts, ps)]
    s = INV_BASE
    while s < c:
        sub = (ri // (2 * s) == cj // (2 * s)) & ((ri // s) % 2 == 1) & ((cj // s) % 2 == 0)
        ets = [mm(jnp.where(sub, lw, 0.0), t) for lw, t in zip(lws, ts)]
        ts = [t - mm(t, et) for t, et in zip(ts, ets)]
        s *= 2
    return ts


def _mixer_kernel(x_ref, pv_ref, win_ref, wri_ref, wout_ref, s0_ref, h0_ref, l0_ref, c0_ref,
                  xo_ref, s_ref, hs_ref, l_ref, co_ref,
                  cbuf, act, gates, scal, obuf, mixo, *, chunk, tb, bb):
    c = chunk
    nck = tb // c
    t_idx = pl.program_id(1)

    @pl.when(t_idx == 0)
    def _():
        s_ref[...] = s0_ref[...]
        hs_ref[...] = h0_ref[...]
        l_ref[...] = l0_ref[...]
        cbuf[:, 0:SUBLANE, :] = c0_ref[...]

    misc = pv_ref[PV_MISC:PV_MISC + 1, :]
    lane1 = lax.broadcasted_iota(jnp.int32, (1, LANE), 1)
    lo = lane1 < HEAD
    lo2 = (lax.broadcasted_iota(jnp.int32, (1, 2 * LANE), 1) % LANE) < HEAD
    dec_lanes = (lane1 >= ALPHA_L) & (lane1 < DT_L + B_HEADS)
    nega = jnp.where(dec_lanes, -jnp.exp(misc[:, M_ALOG:M_ALOG + LANE]), 0.0)
    r128 = lax.broadcasted_iota(jnp.int32, (LANE, LANE), 0)
    c128 = lax.broadcasted_iota(jnp.int32, (LANE, LANE), 1)
    same_head = (r128 // HEAD) == (c128 // HEAD)
    ones_head = jnp.where(same_head, 1.0, 0.0).astype(BF16)
    row_lo = r128 < HEAD

    def head_sum(v):
        return _dot_right_exact(v, ones_head, 2)

    first = SUBLANE - (CONV_W - 1)
    for sq in range(bb):
        hb = _rms(x_ref[sq], pv_ref[PV_NORM:PV_NORM + 1, 0:D_MODEL]).astype(BF16)
        cbuf[sq, SUBLANE:SUBLANE + tb, :] = _dot(hb, win_ref[:, 0:CONV_CH])
        gates[sq] = _dot(hb, win_ref[:, CONV_CH:D_IN_R])
    for sq in range(bb):
        for j in range(CONV_CH // LANE):
            cs = slice(j * LANE, (j + 1) * LANE)
            acc = cbuf[sq, first:first + tb, cs] * pv_ref[PV_CW:PV_CW + 1, cs]
            for k in range(1, CONV_W):
                acc = acc + cbuf[sq, first + k:first + k + tb, cs] * pv_ref[PV_CW + k:PV_CW + k + 1, cs]
            if j * LANE >= XBC_OFF:
                acc = acc + pv_ref[PV_CB:PV_CB + 1, cs]
            if j * LANE < XC_OFF:
                acc = _silu(acc)
            if j * LANE < V_OFF:
                acc = acc * lax.rsqrt(head_sum(acc * acc) + EPS)
                if j * LANE < K_OFF:
                    acc = acc * (A_DK ** -0.5)
            act[sq, :, cs] = acc
        tail = cbuf[sq, tb:tb + SUBLANE, :]
        co_ref[sq] = tail
        cbuf[sq, 0:SUBLANE, :] = tail
        sm = gates[sq, :, SM_OFF:SM_OFF + LANE]
        sp = _softplus(sm + misc[:, M_BIAS:M_BIAS + LANE])
        scal[sq, 0] = _sigmoid(sm)
        scal[sq, 1] = sp
        scal[sq, 2] = sp * nega

    ri = lax.broadcasted_iota(jnp.int32, (c, 2 * c), 0)
    cjw = lax.broadcasted_iota(jnp.int32, (c, 2 * c), 1)
    first_w = cjw < c
    cj = cjw % c
    causal_w = ri >= cj
    strict_w = ri > cj
    tril = jnp.where(lax.broadcasted_iota(jnp.int32, (c, c), 0) >= lax.broadcasted_iota(jnp.int32, (c, c), 1),
                     1.0, 0.0).astype(BF16)

    items = [(sq, ck) for sq in range(bb) for ck in range(nck)]
    rows = {it: slice(it[1] * c, (it[1] + 1) * c) for it in items}

    sc = {}
    for it in items:
        sq, rs = it[0], rows[it]
        g = _dot_left_exact(tril, scal[sq, 2, rs, :], 3)
        gt2 = jnp.concatenate([g, g], axis=0).T
        glast = g[c - 1:c, :]
        sc[it] = dict(g=g, gt2=gt2, eg=jnp.exp(g), egl=jnp.exp(glast - g), gtot=jnp.exp(glast))

    def pairvec(arr, la_, lb_):
        return jnp.where(lo, arr[:, la_:la_ + 1], arr[:, lb_:lb_ + 1])

    def decay_w(d, la_, lb_):
        gcol = jnp.where(first_w, d['g'][:, la_:la_ + 1], d['g'][:, lb_:lb_ + 1])
        grow = jnp.where(first_w[0:1], d['gt2'][la_:la_ + 1, :], d['gt2'][lb_:lb_ + 1, :])
        return jnp.exp(jnp.where(causal_w, gcol - grow, -jnp.inf))

    def state_scale(d, la_, lb_):
        return jnp.where(row_lo, d['gtot'][:, la_:la_ + 1], d['gtot'][:, lb_:lb_ + 1])

    units = [(it, p) for it in items for p in range(N_PAIR)]

    ga = {}
    for un in units:
        (sq, _), p = un
        rs = rows[un[0]]
        qn = act[sq, rs, p * LANE:(p + 1) * LANE]
        kn = act[sq, rs, K_OFF + p * LANE:K_OFF + (p + 1) * LANE]
        ksplit = _head_split_rows(kn, lo).astype(BF16)
        kq = _dot_nt(jnp.concatenate([kn.astype(BF16), qn.astype(BF16)], axis=0), ksplit)
        ga[un] = dict(qn=qn, kn=kn, kq=kq)
    lws = []
    for un in units:
        (sq, _), p = un
        rs, d, a = rows[un[0]], sc[un[0]], ga[un]
        ha, hb_ = 2 * p, 2 * p + 1
        gam = decay_w(d, ALPHA_L + ha, ALPHA_L + hb_)
        beta_c = scal[sq, 0, rs, :]
        bcol = jnp.where(first_w, beta_c[:, BETA_L + ha:BETA_L + ha + 1], beta_c[:, BETA_L + hb_:BETA_L + hb_ + 1])
        lws.append(jnp.where(strict_w, bcol * a['kq'][:c] * gam, 0.0))
        a['qk'] = (a['kq'][c:] * gam).astype(BF16)
        bvec = pairvec(beta_c, BETA_L + ha, BETA_L + hb_)
        egv = pairvec(d['eg'], ALPHA_L + ha, ALPHA_L + hb_)
        eglv = pairvec(d['egl'], ALPHA_L + ha, ALPHA_L + hb_)
        vp = act[sq, rs, V_OFF + p * LANE:V_OFF + (p + 1) * LANE]
        rhs = jnp.concatenate([bvec * vp, bvec * egv * a['kn']], axis=1)
        a['rhs'] = _head_split_rows(rhs, lo2).astype(BF16)
        a['qd'] = a['qn'] * egv
        a['kd'] = (a['kn'] * eglv).astype(BF16)
        del a['kq']

    gb = {}
    for it in items:
        sq, rs = it[0], rows[it]
        bm = act[sq, rs, BM_OFF:BM_OFF + LANE]
        cm = act[sq, rs, CM_OFF:CM_OFF + LANE]
        bm_sw = pltpu.roll(bm, HEAD, 1)
        cm_sw = pltpu.roll(cm, HEAD, 1)
        for p in range(N_PAIR):
            ha, hb_ = 2 * p, 2 * p + 1
            g0, g1 = ha // (B_HEADS // B_GROUPS), hb_ // (B_HEADS // B_GROUPS)
            if g0 == g1 == 0:
                bsel, csel = jnp.where(lo, bm, bm_sw), jnp.where(lo, cm, cm_sw)
            elif g0 == g1 == 1:
                bsel, csel = jnp.where(lo, bm_sw, bm), jnp.where(lo, cm_sw, cm)
            else:
                bsel, csel = bm, cm
            gb[(it, p)] = dict(bsel=bsel, csel=csel,
                               scores=_dot_nt(csel.astype(BF16), _head_split_rows(bsel, lo).astype(BF16)))
    for un in units:
        (sq, _), p = un
        rs, d, b = rows[un[0]], sc[un[0]], gb[un]
        ha, hb_ = 2 * p, 2 * p + 1
        xs = act[sq, rs, XBC_OFF + p * LANE:XBC_OFF + (p + 1) * LANE]
        xdt = xs * pairvec(scal[sq, 1, rs, :], DT_L + ha, DT_L + hb_)
        b['xb'] = xdt.astype(BF16)
        mw = (b['scores'] * decay_w(d, DT_L + ha, DT_L + hb_)).astype(BF16)
        b['y'] = _dot(mw, _head_split_rows(xdt, lo).astype(BF16))
        b['cd'] = (b['csel'] * pairvec(d['eg'], DT_L + ha, DT_L + hb_)).astype(BF16)
        b['bd'] = (b['bsel'] * pairvec(d['egl'], DT_L + ha, DT_L + hb_)).astype(BF16)
        b['xs'] = xs
        del b['scores'], b['csel'], b['bsel']

    tws = _tri_inv_wide(lws, c)
    for un, tw in zip(units, tws):
        ga[un]['uw'] = _dot(tw.astype(BF16), ga[un]['rhs'])

    for ck in range(nck):
        cur = [((sq, ck), p) for sq in range(bb) for p in range(N_PAIR)]
        s_old = {un: s_ref[un[0][0], un[1]] for un in cur}
        h_old = {un: hs_ref[un[0][0], un[1]] for un in cur}
        wq = {un: _dot(jnp.concatenate([ga[un]['uw'][:, LANE:].astype(BF16), ga[un]['qd'].astype(BF16)], axis=0),
                       s_old[un].astype(BF16)) for un in cur}
        yi = {un: _dot(gb[un]['cd'], h_old[un].astype(BF16)) for un in cur}
        for un in cur:
            (sq, _), p = un
            rs, d, a, b = rows[un[0]], sc[un[0]], ga[un], gb[un]
            ha, hb_ = 2 * p, 2 * p + 1
            delta = a['uw'][:, :LANE] - wq[un][:c]
            o = wq[un][c:] + _dot(a['qk'], _head_split_rows(delta, lo).astype(BF16))
            upd = _dot_tn(a['kd'], delta.astype(BF16))
            s_ref[sq, p] = state_scale(d, ALPHA_L + ha, ALPHA_L + hb_) * s_old[un] + jnp.where(same_head, upd, 0.0)
            obuf[sq, rs, p * LANE:(p + 1) * LANE] = o
            updh = _dot_tn(b['bd'], b['xb'])
            hs_ref[sq, p] = state_scale(d, DT_L + ha, DT_L + hb_) * h_old[un] + jnp.where(same_head, updh, 0.0)
            y = b['y'] + yi[un] + misc[:, M_DSKIP + p * LANE:M_DSKIP + (p + 1) * LANE] * b['xs']
            zb = gates[sq, rs, ZB_OFF + p * LANE:ZB_OFF + (p + 1) * LANE]
            obuf[sq, rs, A_WIDTH + p * LANE:A_WIDTH + (p + 1) * LANE] = y * _silu(zb)

    g3r = lax.broadcasted_iota(jnp.int32, (B_WIDTH, B_WIDTH), 0) // (B_WIDTH // B_GROUPS)
    g3c = lax.broadcasted_iota(jnp.int32, (B_WIDTH, B_WIDTH), 1) // (B_WIDTH // B_GROUPS)
    ones_group = jnp.where(g3r == g3c, 1.0, 0.0).astype(BF16)
    rowt = lax.broadcasted_iota(jnp.int32, (tb, LANE), 0)
    for sq in range(bb):
        ms_ = slice(sq * tb, (sq + 1) * tb)
        for p in range(N_PAIR):
            o = obuf[sq, :, p * LANE:(p + 1) * LANE]
            ms = head_sum(o * o) * (1.0 / A_DV)
            on = o * lax.rsqrt(ms + EPS) * misc[:, M_NA:M_NA + LANE]
            za = gates[sq, :, ZA_OFF + p * LANE:ZA_OFF + (p + 1) * LANE]
            mixo[ms_, p * LANE:(p + 1) * LANE] = (on * _silu(za)).astype(BF16)
        yb = obuf[sq, :, A_WIDTH:A_WIDTH + B_WIDTH]
        ms = _dot_right_exact(yb * yb, ones_group, 2) * (1.0 / (B_WIDTH // B_GROUPS))
        ob = yb * lax.rsqrt(ms + EPS) * misc[:, M_NB:M_NB + B_WIDTH]
        mixo[ms_, A_WIDTH:A_WIDTH + B_WIDTH] = ob.astype(BF16)

        xc = act[sq, :, XC_OFF:XC_OFF + C_WIDTH]
        rig = _dot(xc.astype(BF16), wri_ref[...]) + misc[:, M_BRI:M_BRI + 2 * C_WIDTH]
        for half in range(C_WIDTH // LANE):
            hs_ = slice(half * LANE, (half + 1) * LANE)
            xch = xc[:, hs_]
            lam = misc[:, M_LAM + half * LANE:M_LAM + (half + 1) * LANE]
            log_a = -LRU_C * _sigmoid(rig[:, hs_]) * _softplus(-lam)
            a = jnp.exp(log_a)
            b = jnp.sqrt(1.0 - jnp.exp(2.0 * log_a)) * (
                _sigmoid(rig[:, C_WIDTH + half * LANE:C_WIDTH + (half + 1) * LANE]) * xch)
            d = 1
            while d < tb:
                keep = rowt >= d
                a_sh = jnp.where(keep, pltpu.roll(a, d, 0), 1.0)
                b_sh = jnp.where(keep, pltpu.roll(b, d, 0), 0.0)
                b = a * b_sh + b
                a = a * a_sh
                d *= 2
            hseq = a * l_ref[sq, :, hs_] + b
            l_ref[sq, :, hs_] = hseq[tb - 1:tb, :]
            gc = gates[sq, :, GC_OFF + half * LANE:GC_OFF + (half + 1) * LANE]
            mixo[ms_, A_WIDTH + B_WIDTH + half * LANE:A_WIDTH + B_WIDTH + (half + 1) * LANE] = (
                hseq * _gelu_tanh(gc)).astype(BF16)

    mix = _dot(mixo[...], wout_ref[...])
    for sq in range(bb):
        xo_ref[sq] = x_ref[sq] + mix[sq * tb:(sq + 1) * tb]


def _mixer(x, pv, win, wri, wout, s0, h0, l0, c0, *, chunk, tb, bb):
    bsz, l, _ = x.shape
    assert l % tb == 0 and tb % chunk == 0 and bsz % bb == 0
    seq = lambda b, t: (b, 0, 0)
    seq4 = lambda b, t: (b, 0, 0, 0)
    in_specs = [
        pl.BlockSpec((bb, tb, D_MODEL), lambda b, t: (b, t, 0)),
        _resident((PV_ROWS, CONV_CH)),
        _resident((D_MODEL, D_IN_R)),
        _resident((C_WIDTH, 2 * C_WIDTH)),
        _resident((D_MIX, D_MODEL)),
        pl.BlockSpec((bb, N_PAIR, LANE, LANE), seq4),
        pl.BlockSpec((bb, N_PAIR, LANE, LANE), seq4),
        pl.BlockSpec((bb, 1, C_WIDTH), seq),
        pl.BlockSpec((bb, SUBLANE, CONV_CH), seq),
    ]
    out_specs = [
        pl.BlockSpec((bb, tb, D_MODEL), lambda b, t: (b, t, 0)),
        pl.BlockSpec((bb, N_PAIR, LANE, LANE), seq4),
        pl.BlockSpec((bb, N_PAIR, LANE, LANE), seq4),
        pl.BlockSpec((bb, 1, C_WIDTH), seq),
        pl.BlockSpec((bb, SUBLANE, CONV_CH), seq),
    ]
    out_shape = [
        jax.ShapeDtypeStruct((bsz, l, D_MODEL), F32),
        jax.ShapeDtypeStruct((bsz, N_PAIR, LANE, LANE), F32),
        jax.ShapeDtypeStruct((bsz, N_PAIR, LANE, LANE), F32),
        jax.ShapeDtypeStruct((bsz, 1, C_WIDTH), F32),
        jax.ShapeDtypeStruct((bsz, SUBLANE, CONV_CH), F32),
    ]
    scratch = [
        pltpu.VMEM((bb, tb + SUBLANE, CONV_CH), F32),
        pltpu.VMEM((bb, tb, CONV_CH), F32),
        pltpu.VMEM((bb, tb, GATE_CH), F32),
        pltpu.VMEM((bb, 3, tb, LANE), F32),
        pltpu.VMEM((bb, tb, A_WIDTH + B_WIDTH), F32),
        pltpu.VMEM((bb * tb, D_MIX), BF16),
    ]
    return pl.pallas_call(
        functools.partial(_mixer_kernel, chunk=chunk, tb=tb, bb=bb),
        out_shape=out_shape,
        grid=(bsz // bb, l // tb),
        in_specs=in_specs,
        out_specs=out_specs,
        scratch_shapes=scratch,
        compiler_params=pltpu.CompilerParams(dimension_semantics=("arbitrary", "arbitrary"),
                                             vmem_limit_bytes=VMEM_LIMIT),
        name="mixer_c%d" % chunk,
    )(x, pv, win, wri, wout, s0, h0, l0, c0)


def _pair_states(s):
    bsz = s.shape[0]
    s = s.reshape(bsz, N_PAIR, 2, HEAD, HEAD)
    z = jnp.zeros((bsz, N_PAIR, HEAD, HEAD), s.dtype)
    top = jnp.concatenate([s[:, :, 0], z], axis=-1)
    bot = jnp.concatenate([z, s[:, :, 1]], axis=-1)
    return jnp.concatenate([top, bot], axis=-2)


def _unpair_states(sp):
    a = sp[:, :, :HEAD, :HEAD]
    b = sp[:, :, HEAD:, HEAD:]
    return jnp.stack([a, b], axis=2).reshape(sp.shape[0], A_HEADS, HEAD, HEAD)


def _layer_params(lp):
    w = lp['w_in']
    o = 0
    cols = {}
    for name, size in (('qkv', A_CONV_CH), ('z_a', A_WIDTH), ('b_a', A_HEADS), ('a_a', A_HEADS), ('z_b', B_WIDTH),
                       ('xbc', B_CONV_CH), ('dt', B_HEADS), ('gate_c', C_WIDTH), ('x_c', C_WIDTH)):
        cols[name] = w[:, o:o + size]
        o += size
    pad = jnp.zeros((D_MODEL, LANE - 2 * A_HEADS - B_HEADS), w.dtype)
    win = jnp.concatenate([cols['qkv'], cols['xbc'], cols['x_c'], cols['z_a'], cols['z_b'], cols['gate_c'],
                           cols['b_a'], cols['a_a'], cols['dt'], pad], axis=1).astype(BF16)

    def lane_block(*pieces):
        v = jnp.zeros((LANE,), F32)
        for off, val in pieces:
            v = lax.dynamic_update_slice(v, val.astype(F32), (off,))
        return v

    misc = jnp.concatenate([
        lane_block((ALPHA_L, lp['dt_bias_a']), (DT_L, lp['dt_bias_b'])),
        lane_block((ALPHA_L, lp['a_log_a']), (DT_L, lp['a_log_b'])),
        jnp.tile(lp['norm_a_w'], 2),
        jnp.repeat(lp['d_skip_b'], B_HEADDIM),
        lp['norm_b_w'],
        lp['lru_lambda'],
        lp['b_rgate'], lp['b_igate'],
    ])
    misc = jnp.pad(misc, (0, CONV_CH - misc.shape[0]))
    cw = jnp.concatenate([lp['conv_a_w'], lp['conv_b_w'], lp['conv_c_w']], axis=1)
    cb = jnp.concatenate([jnp.zeros((A_CONV_CH,), F32), lp['conv_b_b'], lp['conv_c_b']])
    pv = jnp.concatenate([jnp.pad(lp['norm_mix'], (0, CONV_CH - D_MODEL))[None], cw, cb[None], misc[None],
                          jnp.zeros((1, CONV_CH), F32)], axis=0)

    def block_diag(wb):
        eye = jnp.eye(C_BLOCKS, dtype=wb.dtype)
        return jnp.einsum('ncd,nm->ncmd', wb, eye).reshape(C_WIDTH, C_WIDTH)

    wri = jnp.concatenate([block_diag(lp['w_rgate']), block_diag(lp['w_igate'])], axis=1).astype(BF16)
    return dict(
        pv=pv, win=win, wri=wri, wout=lp['w_out'].astype(BF16),
        n1=lp['norm_ffn1'][None], g1=lp['ffn1_w_gate'].astype(BF16), u1=lp['ffn1_w_up'].astype(BF16),
        d1=lp['ffn1_w_down'].astype(BF16),
        n2=lp['norm_ffn2'][None], g2=lp['ffn2_w_gate'].astype(BF16), u2=lp['ffn2_w_up'].astype(BF16),
        d2=lp['ffn2_w_down'].astype(BF16),
    )


def _trunk(x, states, layers, norm_final, *, chunk, tb, bb, tm):
    bsz, l, _ = x.shape
    delta_s, delta_conv, ssd_h, ssd_conv, lru_h, lru_conv = states
    outs = [[] for _ in range(6)]
    for layer in range(DEPTH):
        lw = layers[layer]
        x2 = _ffn(x.reshape(bsz * l, D_MODEL), lw['n1'], lw['g1'], lw['u1'], lw['d1'], tm=tm)
        conv0 = jnp.concatenate([delta_conv[layer], ssd_conv[layer], lru_conv[layer]], axis=-1).astype(F32)
        conv0 = jnp.pad(conv0, ((0, 0), (SUBLANE - (CONV_W - 1), 0), (0, 0)))
        x3, s_new, h_new, l_new, c_new = _mixer(
            x2.reshape(bsz, l, D_MODEL), lw['pv'], lw['win'], lw['wri'], lw['wout'],
            _pair_states(delta_s[layer].astype(F32)), _pair_states(ssd_h[layer].astype(F32)),
            lru_h[layer].astype(F32)[:, None, :], conv0, chunk=chunk, tb=tb, bb=bb)
        fin = norm_final[None] if layer == DEPTH - 1 else None
        x = _ffn(x3.reshape(bsz * l, D_MODEL), lw['n2'], lw['g2'], lw['u2'], lw['d2'], fin, tm=tm).reshape(bsz, l, D_MODEL)
        c_new = c_new[:, SUBLANE - (CONV_W - 1):, :]
        outs[0].append(_unpair_states(s_new))
        outs[1].append(c_new[..., :A_CONV_CH])
        outs[2].append(_unpair_states(h_new))
        outs[3].append(c_new[..., XBC_OFF:XBC_OFF + B_CONV_CH])
        outs[4].append(l_new[:, 0, :])
        outs[5].append(c_new[..., XC_OFF:])
    return x, tuple(jnp.stack(o) for o in outs)


def kernel(x_prompt, x_sample, state_delta_s, state_delta_conv, state_ssd_h, state_ssd_conv, state_lru_h, state_lru_conv, norm_ffn1, ffn1_w_gate, ffn1_w_up, ffn1_w_down, norm_mix, w_in, conv_a_w, a_log_a, dt_bias_a, norm_a_w, conv_b_w, conv_b_b, a_log_b, dt_bias_b, d_skip_b, norm_b_w, conv_c_w, conv_c_b, w_rgate, b_rgate, w_igate, b_igate, lru_lambda, w_out, norm_ffn2, ffn2_w_gate, ffn2_w_up, ffn2_w_down, norm_final):
    params = {
        'norm_ffn1': norm_ffn1, 'ffn1_w_gate': ffn1_w_gate, 'ffn1_w_up': ffn1_w_up, 'ffn1_w_down': ffn1_w_down,
        'norm_mix': norm_mix, 'w_in': w_in,
        'conv_a_w': conv_a_w, 'a_log_a': a_log_a, 'dt_bias_a': dt_bias_a, 'norm_a_w': norm_a_w,
        'conv_b_w': conv_b_w, 'conv_b_b': conv_b_b, 'a_log_b': a_log_b, 'dt_bias_b': dt_bias_b,
        'd_skip_b': d_skip_b, 'norm_b_w': norm_b_w,
        'conv_c_w': conv_c_w, 'conv_c_b': conv_c_b, 'w_rgate': w_rgate, 'b_rgate': b_rgate,
        'w_igate': w_igate, 'b_igate': b_igate, 'lru_lambda': lru_lambda,
        'w_out': w_out,
        'norm_ffn2': norm_ffn2, 'ffn2_w_gate': ffn2_w_gate, 'ffn2_w_up': ffn2_w_up, 'ffn2_w_down': ffn2_w_down,
    }
    layers = [_layer_params({k: v[i] for k, v in params.items()}) for i in range(DEPTH)]
    bp = x_prompt.shape[0]
    zero_states = (
        jnp.zeros((DEPTH, bp, A_HEADS, A_DK, A_DV), F32),
        jnp.zeros((DEPTH, bp, CONV_W - 1, A_CONV_CH), F32),
        jnp.zeros((DEPTH, bp, B_HEADS, B_STATE, B_HEADDIM), F32),
        jnp.zeros((DEPTH, bp, CONV_W - 1, B_CONV_CH), F32),
        jnp.zeros((DEPTH, bp, C_WIDTH), F32),
        jnp.zeros((DEPTH, bp, CONV_W - 1, C_WIDTH), F32),
    )
    y_prompt, p_states = _trunk(x_prompt, zero_states, layers, norm_final, chunk=128, tb=256, bb=1, tm=512)
    sample_states = (state_delta_s, state_delta_conv, state_ssd_h, state_ssd_conv, state_lru_h, state_lru_conv)
    y_sample, s_states = _trunk(x_sample, sample_states, layers, norm_final, chunk=64, tb=64, bb=4, tm=512)
    return (y_prompt, y_sample) + tuple(p_states) + tuple(s_states)
```

```python
import functools

import jax
import jax.numpy as jnp
from jax import lax
from jax.experimental import pallas as pl
from jax.experimental.pallas import tpu as pltpu

F32 = jnp.float32
BF16 = jnp.bfloat16

D_MODEL = 1024
DEPTH = 2
D_FF = 2816
EPS = 1e-6
CONV_W = 4
A_HEADS = 6
A_DK = 64
A_DV = 64
A_WIDTH = A_HEADS * A_DV
A_CONV_CH = 2 * A_HEADS * A_DK + A_WIDTH
B_HEADS = 6
B_HEADDIM = 64
B_WIDTH = B_HEADS * B_HEADDIM
B_GROUPS = 2
B_STATE = 64
B_CONV_CH = B_WIDTH + 2 * B_GROUPS * B_STATE
C_WIDTH = 256
C_BLOCKS = 8
C_BLOCK = C_WIDTH // C_BLOCKS
LRU_C = 8.0
D_MIX = A_WIDTH + B_WIDTH + C_WIDTH

LANE = 128
SUBLANE = 8
MXU_TILE = 256
assert D_FF % MXU_TILE == 0
HEAD = 64
N_PAIR = A_HEADS // 2
INV_BASE = 16

CONV_CH = A_CONV_CH + B_CONV_CH + C_WIDTH
K_OFF = A_HEADS * A_DK
V_OFF = 2 * A_HEADS * A_DK
XBC_OFF = A_CONV_CH
BM_OFF = XBC_OFF + B_WIDTH
CM_OFF = BM_OFF + B_GROUPS * B_STATE
XC_OFF = A_CONV_CH + B_CONV_CH
GATE_CH = A_WIDTH + B_WIDTH + C_WIDTH + LANE
ZA_OFF, ZB_OFF, GC_OFF, SM_OFF = 0, A_WIDTH, A_WIDTH + B_WIDTH, A_WIDTH + B_WIDTH + C_WIDTH
D_IN_R = CONV_CH + GATE_CH
BETA_L, ALPHA_L, DT_L = 0, A_HEADS, 2 * A_HEADS
PV_ROWS = 8
PV_NORM, PV_CW, PV_CB, PV_MISC = 0, 1, 5, 6
M_BIAS, M_ALOG, M_NA, M_DSKIP, M_NB, M_LAM, M_BRI = 0, 128, 256, 384, 768, 1152, 1408

VMEM_LIMIT = 56 * 1024 * 1024


def _dot(a, b):
    return jnp.dot(a, b, preferred_element_type=F32)


def _dot_nt(a, b):
    return lax.dot_general(a, b, (((1,), (1,)), ((), ())), preferred_element_type=F32)


def _dot_tn(a, b):
    return lax.dot_general(a, b, (((0,), (0,)), ((), ())), preferred_element_type=F32)


def _split(x, n):
    parts = []
    r = x
    for _ in range(n):
        p = r.astype(BF16)
        parts.append(p)
        r = r - p.astype(F32)
    return parts


def _dot_left_exact(m, x, n):
    return sum(_dot(m, p) for p in _split(x, n))


def _dot_right_exact(x, m, n):
    return sum(_dot(p, m) for p in _split(x, n))


def _sigmoid(x):
    return jax.nn.sigmoid(x)


def _silu(x):
    return x * _sigmoid(x)


def _softplus(x):
    return jnp.maximum(x, 0.0) + jnp.log1p(jnp.exp(-jnp.abs(x)))


def _gelu_tanh(x):
    return 0.5 * x * (1.0 + jnp.tanh(0.7978845608028654 * (x + 0.044715 * x * x * x)))


def _rms(x, w):
    ms = jnp.mean(x * x, axis=-1, keepdims=True)
    return x * lax.rsqrt(ms + EPS) * w


def _ffn_body(x_ref, nw_ref, wg_ref, wu_ref, wd_ref, fnw_ref, o_ref, n_split):
    x = x_ref[...]
    h = _rms(x, nw_ref[...]).astype(BF16)
    tiles = D_FF // MXU_TILE
    bounds = [MXU_TILE * ((tiles * j + n_split - 1) // n_split) for j in range(n_split)] + [D_FF]
    acc = None
    for lo_, hi_ in zip(bounds[:-1], bounds[1:]):
        g = _dot(h, wg_ref[:, lo_:hi_])
        u = _dot(h, wu_ref[:, lo_:hi_])
        a = (_silu(g) * u).astype(BF16)
        d = _dot(a, wd_ref[lo_:hi_, :])
        acc = d if acc is None else acc + d
    y = x + 0.5 * acc
    if fnw_ref is not None:
        y = _rms(y, fnw_ref[...])
    o_ref[...] = y


def _ffn_kernel(x_ref, nw_ref, wg_ref, wu_ref, wd_ref, o_ref, *, n_split):
    _ffn_body(x_ref, nw_ref, wg_ref, wu_ref, wd_ref, None, o_ref, n_split)


def _ffn_final_kernel(x_ref, nw_ref, wg_ref, wu_ref, wd_ref, fnw_ref, o_ref, *, n_split):
    _ffn_body(x_ref, nw_ref, wg_ref, wu_ref, wd_ref, fnw_ref, o_ref, n_split)


def _resident(shape):
    return pl.BlockSpec(shape, lambda *_: (0,) * len(shape), pipeline_mode=pl.Buffered(1))


def _ffn(x2d, nw, wg, wu, wd, final_w=None, *, tm):
    t = x2d.shape[0]
    assert t % tm == 0
    row = pl.BlockSpec((tm, D_MODEL), lambda i: (i, 0))
    in_specs = [row, _resident((1, D_MODEL)), _resident((D_MODEL, D_FF)), _resident((D_MODEL, D_FF)),
                _resident((D_FF, D_MODEL))]
    args = [x2d, nw, wg, wu, wd]
    if final_w is None:
        body = functools.partial(_ffn_kernel, n_split=2)
    else:
        body = functools.partial(_ffn_final_kernel, n_split=2)
        in_specs.append(_resident((1, D_MODEL)))
        args.append(final_w)
    return pl.pallas_call(
        body,
        out_shape=jax.ShapeDtypeStruct((t, D_MODEL), F32),
        grid=(t // tm,),
        in_specs=in_specs,
        out_specs=row,
        compiler_params=pltpu.CompilerParams(dimension_semantics=("arbitrary",), vmem_limit_bytes=VMEM_LIMIT),
        name="ffn_final" if final_w is not None else "ffn",
    )(*args)


def _wide_blockdiag(xw, c):
    if c % LANE == 0:
        z = jnp.zeros((c, c), BF16)
        xb = xw.astype(BF16)
        return jnp.concatenate([jnp.concatenate([xb[:, :c], z], axis=1),
                                jnp.concatenate([z, xb[:, c:]], axis=1)], axis=0)
    first = lax.broadcasted_iota(jnp.int32, (1, 2 * c), 1) < c
    return jnp.concatenate([jnp.where(first, xw, 0.0).astype(BF16), jnp.where(first, 0.0, xw).astype(BF16)], axis=0)


def _head_split_rows(x, lo):
    return jnp.concatenate([jnp.where(lo, x, 0.0), jnp.where(lo, 0.0, x)], axis=0)


def _tri_inv_wide(lws, c, between):
    ri = lax.broadcasted_iota(jnp.int32, (c, 2 * c), 0)
    cj = lax.broadcasted_iota(jnp.int32, (c, 2 * c), 1) % c
    diag_blk = ri // INV_BASE == cj // INV_BASE
    eye = jnp.where(ri == cj, 1.0, 0.0)

    def mm(a, bw):
        return _dot(a.astype(BF16), _wide_blockdiag(bw, c))

    ns = [jnp.where(diag_blk, -lw, 0.0) for lw in lws]
    ts = [eye + n for n in ns]
    ps = [mm(n, n) for n in ns]
    between()
    n_pow = 2
    while 2 * n_pow < INV_BASE:
        outs = [_dot(jnp.concatenate([p.astype(BF16), t.astype(BF16)], axis=0),
                     _wide_blockdiag(p, c)) for p, t in zip(ps, ts)]
        between()
        ts = [t + o[c:] for t, o in zip(ts, outs)]
        ps = [o[:c] for o in outs]
        n_pow *= 2
    ts = [t + mm(t, p) for t, p in zip(ts, ps)]
    between()
    s = INV_BASE
    while s < c:
        sub = (ri // (2 * s) == cj // (2 * s)) & ((ri // s) % 2 == 1) & ((cj // s) % 2 == 0)
        ets = [mm(jnp.where(sub, lw, 0.0), t) for lw, t in zip(lws, ts)]
        between()
        ts = [t - mm(t, et) for t, et in zip(ts, ets)]
        between()
        s *= 2
    return ts


def _mixer_kernel(x_ref, xn_ref, pv_ref, win_ref, wri_ref, wout_ref, s0_ref, h0_ref, l0_ref, c0_ref,
                  xo_ref, s_ref, hs_ref, l_ref, co_ref,
                  cbuf, act, gates, gact, scal, obuf, mixo, hbuf, *, chunk, tb, bb, n_t):
    c = chunk
    nck = tb // c
    t_idx = pl.program_id(1)
    n_piece = pl.cdiv(D_IN_R, MXU_TILE)
    conv_pieces = CONV_CH // MXU_TILE

    def project_pieces(src_ref):
        def norm():
            for sq in range(bb):
                hbuf[sq * tb:(sq + 1) * tb, :] = _rms(
                    src_ref[sq], pv_ref[PV_NORM:PV_NORM + 1, 0:D_MODEL]).astype(BF16)

        def piece(j):
            lo_, hi_ = j * MXU_TILE, min((j + 1) * MXU_TILE, D_IN_R)
            res = _dot(hbuf[...], win_ref[:, lo_:hi_])
            for sq in range(bb):
                part = res[sq * tb:(sq + 1) * tb]
                if j < conv_pieces:
                    cbuf[sq, SUBLANE:SUBLANE + tb, lo_:hi_] = part
                else:
                    gates[sq, :, lo_ - CONV_CH:hi_ - CONV_CH] = part

        return [norm] + [functools.partial(piece, j) for j in range(n_piece)]

    @pl.when(t_idx == 0)
    def _():
        s_ref[...] = s0_ref[...]
        hs_ref[...] = h0_ref[...]
        l_ref[...] = l0_ref[...]
        cbuf[:, 0:SUBLANE, :] = c0_ref[...]
        for thunk in project_pieces(x_ref):
            thunk()

    fillers = project_pieces(xn_ref) if n_t > 1 else []

    def fill(n=1):
        for _ in range(n):
            if fillers:
                fillers.pop(0)()

    fill()
    misc = pv_ref[PV_MISC:PV_MISC + 1, :]
    lane1 = lax.broadcasted_iota(jnp.int32, (1, LANE), 1)
    lo = lane1 < HEAD
    lo2 = (lax.broadcasted_iota(jnp.int32, (1, 2 * LANE), 1) % LANE) < HEAD
    dec_lanes = (lane1 >= ALPHA_L) & (lane1 < DT_L + B_HEADS)
    nega = jnp.where(dec_lanes, -jnp.exp(misc[:, M_ALOG:M_ALOG + LANE]), 0.0)
    r128 = lax.broadcasted_iota(jnp.int32, (LANE, LANE), 0)
    c128 = lax.broadcasted_iota(jnp.int32, (LANE, LANE), 1)
    same_head = (r128 // HEAD) == (c128 // HEAD)
    ones_head = jnp.where(same_head, 1.0, 0.0).astype(BF16)
    row_lo = r128 < HEAD

    def head_sum(v):
        return _dot_right_exact(v, ones_head, 1)

    first = SUBLANE - (CONV_W - 1)
    tiles_per_piece = MXU_TILE // LANE
    for j in range(CONV_CH // LANE):
        cs = slice(j * LANE, (j + 1) * LANE)
        for sq in range(bb):
            acc = cbuf[sq, first:first + tb, cs] * pv_ref[PV_CW:PV_CW + 1, cs]
            for k in range(1, CONV_W):
                acc = acc + cbuf[sq, first + k:first + k + tb, cs] * pv_ref[PV_CW + k:PV_CW + k + 1, cs]
            if j * LANE >= XBC_OFF:
                acc = acc + pv_ref[PV_CB:PV_CB + 1, cs]
            if j * LANE < XC_OFF:
                acc = _silu(acc)
            if j * LANE < V_OFF:
                acc = acc * lax.rsqrt(head_sum(acc * acc) + EPS)
                if j * LANE < K_OFF:
                    acc = acc * (A_DK ** -0.5)
            act[sq, :, cs] = acc
        if j % tiles_per_piece == tiles_per_piece - 1:
            ps_ = slice((j + 1 - tiles_per_piece) * LANE, (j + 1) * LANE)
            for sq in range(bb):
                tail = cbuf[sq, tb:tb + SUBLANE, ps_]
                co_ref[sq, :, ps_] = tail
                cbuf[sq, 0:SUBLANE, ps_] = tail
            fill()
    for sq in range(bb):
        for j in range(SM_OFF // LANE):
            cs = slice(j * LANE, (j + 1) * LANE)
            gv = gates[sq, :, cs]
            gact[sq, :, cs] = _silu(gv) if j * LANE < GC_OFF else _gelu_tanh(gv)
        sm = gates[sq, :, SM_OFF:SM_OFF + LANE]
        sp = _softplus(sm + misc[:, M_BIAS:M_BIAS + LANE])
        scal[sq, 0] = _sigmoid(sm)
        scal[sq, 1] = sp
        scal[sq, 2] = sp * nega

    ri = lax.broadcasted_iota(jnp.int32, (c, 2 * c), 0)
    cjw = lax.broadcasted_iota(jnp.int32, (c, 2 * c), 1)
    first_w = cjw < c
    cj = cjw % c
    causal_w = ri >= cj
    strict_w = ri > cj
    tril = jnp.where(lax.broadcasted_iota(jnp.int32, (c, c), 0) >= lax.broadcasted_iota(jnp.int32, (c, c), 1),
                     1.0, 0.0).astype(BF16)

    items = [(sq, ck) for sq in range(bb) for ck in range(nck)]
    rows = {it: slice(it[1] * c, (it[1] + 1) * c) for it in items}

    sc = {}
    for it in items:
        sq, rs = it[0], rows[it]
        g = _dot_left_exact(tril, scal[sq, 2, rs, :], 2)
        gt2 = jnp.concatenate([g, g], axis=0).T
        glast = g[c - 1:c, :]
        sc[it] = dict(g=g, gt2=gt2, eg=jnp.exp(g), egl=jnp.exp(glast - g), gtot=jnp.exp(glast))

    def pairvec(arr, la_, lb_):
        return jnp.where(lo, arr[:, la_:la_ + 1], arr[:, lb_:lb_ + 1])

    def decay_w(d, la_, lb_):
        gcol = jnp.where(first_w, d['g'][:, la_:la_ + 1], d['g'][:, lb_:lb_ + 1])
        grow = jnp.where(first_w[0:1], d['gt2'][la_:la_ + 1, :], d['gt2'][lb_:lb_ + 1, :])
        return jnp.exp(jnp.where(causal_w, gcol - grow, -jnp.inf))

    def state_scale(d, la_, lb_):
        return jnp.where(row_lo, d['gtot'][:, la_:la_ + 1], d['gtot'][:, lb_:lb_ + 1])

    units = [(it, p) for it in items for p in range(N_PAIR)]

    ga = {}
    for un in units:
        (sq, _), p = un
        rs = rows[un[0]]
        qn = act[sq, rs, p * LANE:(p + 1) * LANE]
        kn = act[sq, rs, K_OFF + p * LANE:K_OFF + (p + 1) * LANE]
        ksplit = _head_split_rows(kn, lo).astype(BF16)
        kq = _dot_nt(jnp.concatenate([kn.astype(BF16), qn.astype(BF16)], axis=0), ksplit)
        ga[un] = dict(qn=qn, kn=kn, kq=kq)
    lws = []
    for un in units:
        (sq, _), p = un
        rs, d, a = rows[un[0]], sc[un[0]], ga[un]
        ha, hb_ = 2 * p, 2 * p + 1
        gam = decay_w(d, ALPHA_L + ha, ALPHA_L + hb_)
        beta_c = scal[sq, 0, rs, :]
        bcol = jnp.where(first_w, beta_c[:, BETA_L + ha:BETA_L + ha + 1], beta_c[:, BETA_L + hb_:BETA_L + hb_ + 1])
        lws.append(jnp.where(strict_w, bcol * a['kq'][:c] * gam, 0.0))
        a['qk'] = (a['kq'][c:] * gam).astype(BF16)
        bvec = pairvec(beta_c, BETA_L + ha, BETA_L + hb_)
        egv = pairvec(d['eg'], ALPHA_L + ha, ALPHA_L + hb_)
        eglv = pairvec(d['egl'], ALPHA_L + ha, ALPHA_L + hb_)
        vp = act[sq, rs, V_OFF + p * LANE:V_OFF + (p + 1) * LANE]
        rhs = jnp.concatenate([bvec * vp, bvec * egv * a['kn']], axis=1)
        a['rhs'] = _head_split_rows(rhs, lo2).astype(BF16)
        a['qd'] = a['qn'] * egv
        a['kd'] = (a['kn'] * eglv).astype(BF16)
        del a['kq']

    gb = {}
    for it in items:
        sq, rs = it[0], rows[it]
        bm = act[sq, rs, BM_OFF:BM_OFF + LANE]
        cm = act[sq, rs, CM_OFF:CM_OFF + LANE]
        bm_sw = pltpu.roll(bm, HEAD, 1)
        cm_sw = pltpu.roll(cm, HEAD, 1)
        for p in range(N_PAIR):
            ha, hb_ = 2 * p, 2 * p + 1
            g0, g1 = ha // (B_HEADS // B_GROUPS), hb_ // (B_HEADS // B_GROUPS)
            if g0 == g1 == 0:
                bsel, csel = jnp.where(lo, bm, bm_sw), jnp.where(lo, cm, cm_sw)
            elif g0 == g1 == 1:
                bsel, csel = jnp.where(lo, bm_sw, bm), jnp.where(lo, cm_sw, cm)
            else:
                bsel, csel = bm, cm
            gb[(it, p)] = dict(bsel=bsel, csel=csel,
                               scores=_dot_nt(csel.astype(BF16), _head_split_rows(bsel, lo).astype(BF16)))
    for un in units:
        (sq, _), p = un
        rs, d, b = rows[un[0]], sc[un[0]], gb[un]
        ha, hb_ = 2 * p, 2 * p + 1
        xs = act[sq, rs, XBC_OFF + p * LANE:XBC_OFF + (p + 1) * LANE]
        xdt = xs * pairvec(scal[sq, 1, rs, :], DT_L + ha, DT_L + hb_)
        b['xb'] = xdt.astype(BF16)
        mw = (b['scores'] * decay_w(d, DT_L + ha, DT_L + hb_)).astype(BF16)
        b['y'] = _dot(mw, _head_split_rows(xdt, lo).astype(BF16))
        b['cd'] = (b['csel'] * pairvec(d['eg'], DT_L + ha, DT_L + hb_)).astype(BF16)
        b['bd'] = (b['bsel'] * pairvec(d['egl'], DT_L + ha, DT_L + hb_)).astype(BF16)
        b['xs'] = xs
        del b['scores'], b['csel'], b['bsel']

    tws = _tri_inv_wide(lws, c, fill)
    for un, tw in zip(units, tws):
        ga[un]['uw'] = _dot(tw.astype(BF16), ga[un]['rhs'])

    for ck in range(nck):
        cur = [((sq, ck), p) for sq in range(bb) for p in range(N_PAIR)]
        s_old = {un: s_ref[un[0][0], un[1]] for un in cur}
        h_old = {un: hs_ref[un[0][0], un[1]] for un in cur}
        wq = {un: _dot(jnp.concatenate([ga[un]['uw'][:, LANE:].astype(BF16), ga[un]['qd'].astype(BF16)], axis=0),
                       s_old[un].astype(BF16)) for un in cur}
        yi = {un: _dot(gb[un]['cd'], h_old[un].astype(BF16)) for un in cur}
        for un in cur:
            (sq, _), p = un
            rs, d, a, b = rows[un[0]], sc[un[0]], ga[un], gb[un]
            ha, hb_ = 2 * p, 2 * p + 1
            delta = a['uw'][:, :LANE] - wq[un][:c]
            o = wq[un][c:] + _dot(a['qk'], _head_split_rows(delta, lo).astype(BF16))
            upd = _dot_tn(a['kd'], delta.astype(BF16))
            s_ref[sq, p] = state_scale(d, ALPHA_L + ha, ALPHA_L + hb_) * s_old[un] + jnp.where(same_head, upd, 0.0)
            obuf[sq, rs, p * LANE:(p + 1) * LANE] = o
            updh = _dot_tn(b['bd'], b['xb'])
            hs_ref[sq, p] = state_scale(d, DT_L + ha, DT_L + hb_) * h_old[un] + jnp.where(same_head, updh, 0.0)
            y = b['y'] + yi[un] + misc[:, M_DSKIP + p * LANE:M_DSKIP + (p + 1) * LANE] * b['xs']
            obuf[sq, rs, A_WIDTH + p * LANE:A_WIDTH + (p + 1) * LANE] = (
                y * gact[sq, rs, ZB_OFF + p * LANE:ZB_OFF + (p + 1) * LANE])

    fill(len(fillers))

    g3r = lax.broadcasted_iota(jnp.int32, (B_WIDTH, B_WIDTH), 0) // (B_WIDTH // B_GROUPS)
    g3c = lax.broadcasted_iota(jnp.int32, (B_WIDTH, B_WIDTH), 1) // (B_WIDTH // B_GROUPS)
    ones_group = jnp.where(g3r == g3c, 1.0, 0.0).astype(BF16)
    rowt = lax.broadcasted_iota(jnp.int32, (tb, LANE), 0)
    for sq in range(bb):
        ms_ = slice(sq * tb, (sq + 1) * tb)
        for p in range(N_PAIR):
            o = obuf[sq, :, p * LANE:(p + 1) * LANE]
            ms = head_sum(o * o) * (1.0 / A_DV)
            on = o * lax.rsqrt(ms + EPS) * misc[:, M_NA:M_NA + LANE]
            mixo[ms_, p * LANE:(p + 1) * LANE] = (
                on * gact[sq, :, ZA_OFF + p * LANE:ZA_OFF + (p + 1) * LANE]).astype(BF16)
        yb = obuf[sq, :, A_WIDTH:A_WIDTH + B_WIDTH]
        ms = _dot_right_exact(yb * yb, ones_group, 1) * (1.0 / (B_WIDTH // B_GROUPS))
        ob = yb * lax.rsqrt(ms + EPS) * misc[:, M_NB:M_NB + B_WIDTH]
        mixo[ms_, A_WIDTH:A_WIDTH + B_WIDTH] = ob.astype(BF16)

        xc = act[sq, :, XC_OFF:XC_OFF + C_WIDTH]
        rig = _dot(xc.astype(BF16), wri_ref[...]) + misc[:, M_BRI:M_BRI + 2 * C_WIDTH]
        for half in range(C_WIDTH // LANE):
            hs_ = slice(half * LANE, (half + 1) * LANE)
            xch = xc[:, hs_]
            lam = misc[:, M_LAM + half * LANE:M_LAM + (half + 1) * LANE]
            log_a = -LRU_C * _sigmoid(rig[:, hs_]) * _softplus(-lam)
            a = jnp.exp(log_a)
            b = jnp.sqrt(1.0 - jnp.exp(2.0 * log_a)) * (
                _sigmoid(rig[:, C_WIDTH + half * LANE:C_WIDTH + (half + 1) * LANE]) * xch)
            d = 1
            while d < tb:
                keep = rowt >= d
                a_sh = jnp.where(keep, pltpu.roll(a, d, 0), 1.0)
                b_sh = jnp.where(keep, pltpu.roll(b, d, 0), 0.0)
                b = a * b_sh + b
                a = a * a_sh
                d *= 2
            hseq = a * l_ref[sq, :, hs_] + b
            l_ref[sq, :, hs_] = hseq[tb - 1:tb, :]
            mixo[ms_, A_WIDTH + B_WIDTH + half * LANE:A_WIDTH + B_WIDTH + (half + 1) * LANE] = (
                hseq * gact[sq, :, GC_OFF + half * LANE:GC_OFF + (half + 1) * LANE]).astype(BF16)

    mix = _dot(mixo[...], wout_ref[...])
    for sq in range(bb):
        xo_ref[sq] = x_ref[sq] + mix[sq * tb:(sq + 1) * tb]


def _mixer(x, pv, win, wri, wout, s0, h0, l0, c0, *, chunk, tb, bb):
    bsz, l, _ = x.shape
    assert l % tb == 0 and tb % chunk == 0 and bsz % bb == 0
    n_t = l // tb
    seq = lambda b, t: (b, 0, 0)
    seq4 = lambda b, t: (b, 0, 0, 0)
    in_specs = [
        pl.BlockSpec((bb, tb, D_MODEL), lambda b, t: (b, t, 0)),
        pl.BlockSpec((bb, tb, D_MODEL), lambda b, t: (b, jnp.minimum(t + 1, n_t - 1), 0)),
        _resident((PV_ROWS, CONV_CH)),
        _resident((D_MODEL, D_IN_R)),
        _resident((C_WIDTH, 2 * C_WIDTH)),
        _resident((D_MIX, D_MODEL)),
        pl.BlockSpec((bb, N_PAIR, LANE, LANE), seq4),
        pl.BlockSpec((bb, N_PAIR, LANE, LANE), seq4),
        pl.BlockSpec((bb, 1, C_WIDTH), seq),
        pl.BlockSpec((bb, SUBLANE, CONV_CH), seq),
    ]
    out_specs = [
        pl.BlockSpec((bb, tb, D_MODEL), lambda b, t: (b, t, 0)),
        pl.BlockSpec((bb, N_PAIR, LANE, LANE), seq4),
        pl.BlockSpec((bb, N_PAIR, LANE, LANE), seq4),
        pl.BlockSpec((bb, 1, C_WIDTH), seq),
        pl.BlockSpec((bb, SUBLANE, CONV_CH), seq),
    ]
    out_shape = [
        jax.ShapeDtypeStruct((bsz, l, D_MODEL), F32),
        jax.ShapeDtypeStruct((bsz, N_PAIR, LANE, LANE), F32),
        jax.ShapeDtypeStruct((bsz, N_PAIR, LANE, LANE), F32),
        jax.ShapeDtypeStruct((bsz, 1, C_WIDTH), F32),
        jax.ShapeDtypeStruct((bsz, SUBLANE, CONV_CH), F32),
    ]
    scratch = [
        pltpu.VMEM((bb, tb + SUBLANE, CONV_CH), F32),
        pltpu.VMEM((bb, tb, CONV_CH), F32),
        pltpu.VMEM((bb, tb, GATE_CH), F32),
        pltpu.VMEM((bb, tb, SM_OFF), F32),
        pltpu.VMEM((bb, 3, tb, LANE), F32),
        pltpu.VMEM((bb, tb, A_WIDTH + B_WIDTH), F32),
        pltpu.VMEM((bb * tb, D_MIX), BF16),
        pltpu.VMEM((bb * tb, D_MODEL), BF16),
    ]
    return pl.pallas_call(
        functools.partial(_mixer_kernel, chunk=chunk, tb=tb, bb=bb, n_t=n_t),
        out_shape=out_shape,
        grid=(bsz // bb, l // tb),
        in_specs=in_specs,
        out_specs=out_specs,
        scratch_shapes=scratch,
        compiler_params=pltpu.CompilerParams(dimension_semantics=("arbitrary", "arbitrary"),
                                             vmem_limit_bytes=VMEM_LIMIT),
        name="mixer_c%d" % chunk,
    )(x, x, pv, win, wri, wout, s0, h0, l0, c0)


def _pair_states(s):
    bsz = s.shape[0]
    s = s.reshape(bsz, N_PAIR, 2, HEAD, HEAD)
    z = jnp.zeros((bsz, N_PAIR, HEAD, HEAD), s.dtype)
    top = jnp.concatenate([s[:, :, 0], z], axis=-1)
    bot = jnp.concatenate([z, s[:, :, 1]], axis=-1)
    return jnp.concatenate([top, bot], axis=-2)


def _unpair_states(sp):
    a = sp[:, :, :HEAD, :HEAD]
    b = sp[:, :, HEAD:, HEAD:]
    return jnp.stack([a, b], axis=2).reshape(sp.shape[0], A_HEADS, HEAD, HEAD)


def _layer_params(lp):
    w = lp['w_in']
    o = 0
    cols = {}
    for name, size in (('qkv', A_CONV_CH), ('z_a', A_WIDTH), ('b_a', A_HEADS), ('a_a', A_HEADS), ('z_b', B_WIDTH),
                       ('xbc', B_CONV_CH), ('dt', B_HEADS), ('gate_c', C_WIDTH), ('x_c', C_WIDTH)):
        cols[name] = w[:, o:o + size]
        o += size
    pad = jnp.zeros((D_MODEL, LANE - 2 * A_HEADS - B_HEADS), w.dtype)
    win = jnp.concatenate([cols['qkv'], cols['xbc'], cols['x_c'], cols['z_a'], cols['z_b'], cols['gate_c'],
                           cols['b_a'], cols['a_a'], cols['dt'], pad], axis=1).astype(BF16)

    def lane_block(*pieces):
        v = jnp.zeros((LANE,), F32)
        for off, val in pieces:
            v = lax.dynamic_update_slice(v, val.astype(F32), (off,))
        return v

    misc = jnp.concatenate([
        lane_block((ALPHA_L, lp['dt_bias_a']), (DT_L, lp['dt_bias_b'])),
        lane_block((ALPHA_L, lp['a_log_a']), (DT_L, lp['a_log_b'])),
        jnp.tile(lp['norm_a_w'], 2),
        jnp.repeat(lp['d_skip_b'], B_HEADDIM),
        lp['norm_b_w'],
        lp['lru_lambda'],
        lp['b_rgate'], lp['b_igate'],
    ])
    misc = jnp.pad(misc, (0, CONV_CH - misc.shape[0]))
    cw = jnp.concatenate([lp['conv_a_w'], lp['conv_b_w'], lp['conv_c_w']], axis=1)
    cb = jnp.concatenate([jnp.zeros((A_CONV_CH,), F32), lp['conv_b_b'], lp['conv_c_b']])
    pv = jnp.concatenate([jnp.pad(lp['norm_mix'], (0, CONV_CH - D_MODEL))[None], cw, cb[None], misc[None],
                          jnp.zeros((1, CONV_CH), F32)], axis=0)

    def block_diag(wb):
        eye = jnp.eye(C_BLOCKS, dtype=wb.dtype)
        return jnp.einsum('ncd,nm->ncmd', wb, eye).reshape(C_WIDTH, C_WIDTH)

    wri = jnp.concatenate([block_diag(lp['w_rgate']), block_diag(lp['w_igate'])], axis=1).astype(BF16)
    return dict(
        pv=pv, win=win, wri=wri, wout=lp['w_out'].astype(BF16),
        n1=lp['norm_ffn1'][None], g1=lp['ffn1_w_gate'].astype(BF16), u1=lp['ffn1_w_up'].astype(BF16),
        d1=lp['ffn1_w_down'].astype(BF16),
        n2=lp['norm_ffn2'][None], g2=lp['ffn2_w_gate'].astype(BF16), u2=lp['ffn2_w_up'].astype(BF16),
        d2=lp['ffn2_w_down'].astype(BF16),
    )


def _trunk(x, states, layers, norm_final, *, chunk, tb, bb, tm):
    bsz, l, _ = x.shape
    delta_s, delta_conv, ssd_h, ssd_conv, lru_h, lru_conv = states
    outs = [[] for _ in range(6)]
    for layer in range(DEPTH):
        lw = layers[layer]
        x2 = _ffn(x.reshape(bsz * l, D_MODEL), lw['n1'], lw['g1'], lw['u1'], lw['d1'], tm=tm)
        conv0 = jnp.concatenate([delta_conv[layer], ssd_conv[layer], lru_conv[layer]], axis=-1).astype(F32)
        conv0 = jnp.pad(conv0, ((0, 0), (SUBLANE - (CONV_W - 1), 0), (0, 0)))
        x3, s_new, h_new, l_new, c_new = _mixer(
            x2.reshape(bsz, l, D_MODEL), lw['pv'], lw['win'], lw['wri'], lw['wout'],
            _pair_states(delta_s[layer].astype(F32)), _pair_states(ssd_h[layer].astype(F32)),
            lru_h[layer].astype(F32)[:, None, :], conv0, chunk=chunk, tb=tb, bb=bb)
        fin = norm_final[None] if layer == DEPTH - 1 else None
        x = _ffn(x3.reshape(bsz * l, D_MODEL), lw['n2'], lw['g2'], lw['u2'], lw['d2'], fin, tm=tm).reshape(bsz, l, D_MODEL)
        c_new = c_new[:, SUBLANE - (CONV_W - 1):, :]
        outs[0].append(_unpair_states(s_new))
        outs[1].append(c_new[..., :A_CONV_CH])
        outs[2].append(_unpair_states(h_new))
        outs[3].append(c_new[..., XBC_OFF:XBC_OFF + B_CONV_CH])
        outs[4].append(l_new[:, 0, :])
        outs[5].append(c_new[..., XC_OFF:])
    return x, tuple(jnp.stack(o) for o in outs)


def kernel(x_prompt, x_sample, state_delta_s, state_delta_conv, state_ssd_h, state_ssd_conv, state_lru_h, state_lru_conv, norm_ffn1, ffn1_w_gate, ffn1_w_up, ffn1_w_down, norm_mix, w_in, conv_a_w, a_log_a, dt_bias_a, norm_a_w, conv_b_w, conv_b_b, a_log_b, dt_bias_b, d_skip_b, norm_b_w, conv_c_w, conv_c_b, w_rgate, b_rgate, w_igate, b_igate, lru_lambda, w_out, norm_ffn2, ffn2_w_gate, ffn2_w_up, ffn2_w_down, norm_final):
    params = {
        'norm_ffn1': norm_ffn1, 'ffn1_w_gate': ffn1_w_gate, 'ffn1_w_up': ffn1_w_up, 'ffn1_w_down': ffn1_w_down,
        'norm_mix': norm_mix, 'w_in': w_in,
        'conv_a_w': conv_a_w, 'a_log_a': a_log_a, 'dt_bias_a': dt_bias_a, 'norm_a_w': norm_a_w,
        'conv_b_w': conv_b_w, 'conv_b_b': conv_b_b, 'a_log_b': a_log_b, 'dt_bias_b': dt_bias_b,
        'd_skip_b': d_skip_b, 'norm_b_w': norm_b_w,
        'conv_c_w': conv_c_w, 'conv_c_b': conv_c_b, 'w_rgate': w_rgate, 'b_rgate': b_rgate,
        'w_igate': w_igate, 'b_igate': b_igate, 'lru_lambda': lru_lambda,
        'w_out': w_out,
        'norm_ffn2': norm_ffn2, 'ffn2_w_gate': ffn2_w_gate, 'ffn2_w_up': ffn2_w_up, 'ffn2_w_down': ffn2_w_down,
    }
    layers = [_layer_params({k: v[i] for k, v in params.items()}) for i in range(DEPTH)]
    bp = x_prompt.shape[0]
    zero_states = (
        jnp.zeros((DEPTH, bp, A_HEADS, A_DK, A_DV), F32),
        jnp.zeros((DEPTH, bp, CONV_W - 1, A_CONV_CH), F32),
        jnp.zeros((DEPTH, bp, B_HEADS, B_STATE, B_HEADDIM), F32),
        jnp.zeros((DEPTH, bp, CONV_W - 1, B_CONV_CH), F32),
        jnp.zeros((DEPTH, bp, C_WIDTH), F32),
        jnp.zeros((DEPTH, bp, CONV_W - 1, C_WIDTH), F32),
    )
    y_prompt, p_states = _trunk(x_prompt, zero_states, layers, norm_final, chunk=128, tb=256, bb=1, tm=512)
    sample_states = (state_delta_s, state_delta_conv, state_ssd_h, state_ssd_conv, state_lru_h, state_lru_conv)
    y_sample, s_states = _trunk(x_sample, sample_states, layers, norm_final, chunk=64, tb=64, bb=4, tm=512)
    return (y_prompt, y_sample) + tuple(p_states) + tuple(s_states)
```

```python
import functools

import jax
import jax.numpy as jnp
from jax import lax
from jax.experimental import pallas as pl
from jax.experimental.pallas import tpu as pltpu

F32 = jnp.float32
BF16 = jnp.bfloat16

D_MODEL = 1024
DEPTH = 2
D_FF = 2816
EPS = 1e-6
CONV_W = 4
A_HEADS = 6
A_DK = 64
A_DV = 64
A_WIDTH = A_HEADS * A_DV
A_CONV_CH = 2 * A_HEADS * A_DK + A_WIDTH
B_HEADS = 6
B_HEADDIM = 64
B_WIDTH = B_HEADS * B_HEADDIM
B_GROUPS = 2
B_STATE = 64
B_CONV_CH = B_WIDTH + 2 * B_GROUPS * B_STATE
C_WIDTH = 256
C_BLOCKS = 8
C_BLOCK = C_WIDTH // C_BLOCKS
LRU_C = 8.0
D_MIX = A_WIDTH + B_WIDTH + C_WIDTH

LANE = 128
SUBLANE = 8
MXU_TILE = 256
assert D_FF % MXU_TILE == 0
HEAD = 64
N_PAIR = A_HEADS // 2
INV_BASE = 16

CONV_CH = A_CONV_CH + B_CONV_CH + C_WIDTH
K_OFF = A_HEADS * A_DK
V_OFF = 2 * A_HEADS * A_DK
XBC_OFF = A_CONV_CH
BM_OFF = XBC_OFF + B_WIDTH
CM_OFF = BM_OFF + B_GROUPS * B_STATE
XC_OFF = A_CONV_CH + B_CONV_CH
GATE_CH = A_WIDTH + B_WIDTH + C_WIDTH + LANE
ZA_OFF, ZB_OFF, GC_OFF, SM_OFF = 0, A_WIDTH, A_WIDTH + B_WIDTH, A_WIDTH + B_WIDTH + C_WIDTH
D_IN_R = CONV_CH + GATE_CH
BETA_L, ALPHA_L, DT_L = 0, A_HEADS, 2 * A_HEADS
PV_ROWS = 8
PV_NORM, PV_CW, PV_CB, PV_MISC = 0, 1, 5, 6
M_BIAS, M_ALOG, M_NA, M_DSKIP, M_NB, M_LAM, M_BRI = 0, 128, 256, 384, 768, 1152, 1408

VMEM_LIMIT = 56 * 1024 * 1024


def _dot(a, b):
    return jnp.dot(a, b, preferred_element_type=F32)


def _dot_nt(a, b):
    return lax.dot_general(a, b, (((1,), (1,)), ((), ())), preferred_element_type=F32)


def _dot_tn(a, b):
    return lax.dot_general(a, b, (((0,), (0,)), ((), ())), preferred_element_type=F32)


def _split(x, n):
    parts = []
    r = x
    for _ in range(n):
        p = r.astype(BF16)
        parts.append(p)
        r = r - p.astype(F32)
    return parts


def _dot_left_exact(m, x, n):
    return sum(_dot(m, p) for p in _split(x, n))


def _dot_right_exact(x, m, n):
    return sum(_dot(p, m) for p in _split(x, n))


def _sigmoid(x):
    return jax.nn.sigmoid(x)


def _silu(x):
    return x * _sigmoid(x)


def _softplus(x):
    return jnp.maximum(x, 0.0) + jnp.log1p(jnp.exp(-jnp.abs(x)))


def _gelu_tanh(x):
    return 0.5 * x * (1.0 + jnp.tanh(0.7978845608028654 * (x + 0.044715 * x * x * x)))


def _rms(x, w):
    ms = jnp.mean(x * x, axis=-1, keepdims=True)
    return x * lax.rsqrt(ms + EPS) * w


def _ffn_body(x_ref, nw_ref, wg_ref, wu_ref, wd_ref, fnw_ref, o_ref, n_split):
    x = x_ref[...]
    h = _rms(x, nw_ref[...]).astype(BF16)
    tiles = D_FF // MXU_TILE
    bounds = [MXU_TILE * ((tiles * j + n_split - 1) // n_split) for j in range(n_split)] + [D_FF]
    acc = None
    for lo_, hi_ in zip(bounds[:-1], bounds[1:]):
        g = _dot(h, wg_ref[:, lo_:hi_])
        u = _dot(h, wu_ref[:, lo_:hi_])
        a = (_silu(g) * u).astype(BF16)
        d = _dot(a, wd_ref[lo_:hi_, :])
        acc = d if acc is None else acc + d
    y = x + 0.5 * acc
    if fnw_ref is not None:
        y = _rms(y, fnw_ref[...])
    o_ref[...] = y


def _ffn_kernel(x_ref, nw_ref, wg_ref, wu_ref, wd_ref, o_ref, *, n_split):
    _ffn_body(x_ref, nw_ref, wg_ref, wu_ref, wd_ref, None, o_ref, n_split)


def _ffn_final_kernel(x_ref, nw_ref, wg_ref, wu_ref, wd_ref, fnw_ref, o_ref, *, n_split):
    _ffn_body(x_ref, nw_ref, wg_ref, wu_ref, wd_ref, fnw_ref, o_ref, n_split)


def _resident(shape):
    return pl.BlockSpec(shape, lambda *_: (0,) * len(shape), pipeline_mode=pl.Buffered(1))


def _ffn(x2d, nw, wg, wu, wd, final_w=None, *, tm):
    t = x2d.shape[0]
    assert t % tm == 0
    row = pl.BlockSpec((tm, D_MODEL), lambda i: (i, 0))
    in_specs = [row, _resident((1, D_MODEL)), _resident((D_MODEL, D_FF)), _resident((D_MODEL, D_FF)),
                _resident((D_FF, D_MODEL))]
    args = [x2d, nw, wg, wu, wd]
    if final_w is None:
        body = functools.partial(_ffn_kernel, n_split=2)
    else:
        body = functools.partial(_ffn_final_kernel, n_split=2)
        in_specs.append(_resident((1, D_MODEL)))
        args.append(final_w)
    return pl.pallas_call(
        body,
        out_shape=jax.ShapeDtypeStruct((t, D_MODEL), F32),
        grid=(t // tm,),
        in_specs=in_specs,
        out_specs=row,
        compiler_params=pltpu.CompilerParams(dimension_semantics=("arbitrary",), vmem_limit_bytes=VMEM_LIMIT),
        name="ffn_final" if final_w is not None else "ffn",
    )(*args)


def _wide_blockdiag(xw, c):
    if c % LANE == 0:
        z = jnp.zeros((c, c), BF16)
        xb = xw.astype(BF16)
        return jnp.concatenate([jnp.concatenate([xb[:, :c], z], axis=1),
                                jnp.concatenate([z, xb[:, c:]], axis=1)], axis=0)
    first = lax.broadcasted_iota(jnp.int32, (1, 2 * c), 1) < c
    return jnp.concatenate([jnp.where(first, xw, 0.0).astype(BF16), jnp.where(first, 0.0, xw).astype(BF16)], axis=0)


def _head_split_rows(x, lo):
    return jnp.concatenate([jnp.where(lo, x, 0.0), jnp.where(lo, 0.0, x)], axis=0)


def _tri_inv_wide(lws, c, between):
    ri = lax.broadcasted_iota(jnp.int32, (c, 2 * c), 0)
    cj = lax.broadcasted_iota(jnp.int32, (c, 2 * c), 1) % c
    diag_blk = ri // INV_BASE == cj // INV_BASE
    eye = jnp.where(ri == cj, 1.0, 0.0)

    def mm(a, bw):
        return _dot(a.astype(BF16), _wide_blockdiag(bw, c))

    ns = [jnp.where(diag_blk, -lw, 0.0) for lw in lws]
    ts = [eye + n for n in ns]
    ps = [mm(n, n) for n in ns]
    between()
    n_pow = 2
    while 2 * n_pow < INV_BASE:
        outs = [_dot(jnp.concatenate([p.astype(BF16), t.astype(BF16)], axis=0),
                     _wide_blockdiag(p, c)) for p, t in zip(ps, ts)]
        between()
        ts = [t + o[c:] for t, o in zip(ts, outs)]
        ps = [o[:c] for o in outs]
        n_pow *= 2
    ts = [t + mm(t, p) for t, p in zip(ts, ps)]
    between()
    s = INV_BASE
    while s < c:
        sub = (ri // (2 * s) == cj // (2 * s)) & ((ri // s) % 2 == 1) & ((cj // s) % 2 == 0)
        ets = [mm(jnp.where(sub, lw, 0.0), t) for lw, t in zip(lws, ts)]
        between()
        ts = [t - mm(t, et) for t, et in zip(ts, ets)]
        between()
        s *= 2
    return ts


def _mixer_kernel(x_ref, xn_ref, pv_ref, win_ref, wri_ref, wout_ref, n1_ref, wg_ref, wu_ref, wd_ref,
                  s0_ref, h0_ref, l0_ref, c0_ref,
                  xo_ref, s_ref, hs_ref, l_ref, co_ref,
                  cbuf, act, gates, gact, scal, obuf, mixo, hbuf, hbuf1, gbuf, abuf, facc, x2c, x2n,
                  *, chunk, tb, bb, n_t):
    c = chunk
    nck = tb // c
    t_idx = pl.program_id(1)
    n_piece = pl.cdiv(D_IN_R, MXU_TILE)
    conv_pieces = CONV_CH // MXU_TILE
    ff_tiles = D_FF // MXU_TILE

    def front_pieces(src_ref, x2dst):
        def norm1():
            for sq in range(bb):
                hbuf1[sq * tb:(sq + 1) * tb, :] = _rms(src_ref[sq], n1_ref[...]).astype(BF16)

        def gate(j):
            gbuf[...] = _dot(hbuf1[...], wg_ref[:, j * MXU_TILE:(j + 1) * MXU_TILE])

        def up(j):
            u = _dot(hbuf1[...], wu_ref[:, j * MXU_TILE:(j + 1) * MXU_TILE])
            abuf[j % 2] = (_silu(gbuf[...]) * u).astype(BF16)

        def down(j):
            d = _dot(abuf[j % 2], wd_ref[j * MXU_TILE:(j + 1) * MXU_TILE, :])
            facc[...] = d if j == 0 else facc[...] + d

        def norm2():
            for sq in range(bb):
                x2 = src_ref[sq] + 0.5 * facc[sq * tb:(sq + 1) * tb, :]
                x2dst[sq] = x2
                hbuf[sq * tb:(sq + 1) * tb, :] = _rms(x2, pv_ref[PV_NORM:PV_NORM + 1, 0:D_MODEL]).astype(BF16)

        def piece(j):
            lo_, hi_ = j * MXU_TILE, min((j + 1) * MXU_TILE, D_IN_R)
            res = _dot(hbuf[...], win_ref[:, lo_:hi_])
            for sq in range(bb):
                part = res[sq * tb:(sq + 1) * tb]
                if j < conv_pieces:
                    cbuf[sq, SUBLANE:SUBLANE + tb, lo_:hi_] = part
                else:
                    gates[sq, :, lo_ - CONV_CH:hi_ - CONV_CH] = part

        thunks = [norm1]
        for j in range(ff_tiles):
            thunks += [functools.partial(gate, j), functools.partial(up, j)]
            if j > 0:
                thunks.append(functools.partial(down, j - 1))
        thunks += [functools.partial(down, ff_tiles - 1), norm2]
        return thunks + [functools.partial(piece, j) for j in range(n_piece)]

    @pl.when(t_idx == 0)
    def _():
        s_ref[...] = s0_ref[...]
        hs_ref[...] = h0_ref[...]
        l_ref[...] = l0_ref[...]
        cbuf[:, 0:SUBLANE, :] = c0_ref[...]
        for thunk in front_pieces(x_ref, x2c):
            thunk()

    if n_t > 1:
        @pl.when(t_idx > 0)
        def _():
            x2c[...] = x2n[...]

    fillers = front_pieces(xn_ref, x2n) if n_t > 1 else []
    n_units = bb * nck * N_PAIR
    n_hooks = (1 + CONV_CH // LANE + SM_OFF // LANE + bb * nck + 4 * n_units
               + 4 + 2 * max(0, (c // INV_BASE).bit_length() - 1) + nck * (2 + bb * N_PAIR) + 6 * bb)
    per_hook = len(fillers) / (n_hooks - 4)
    credit = [0.0]

    def fill():
        credit[0] += per_hook
        while credit[0] >= 1.0 and fillers:
            fillers.pop(0)()
            credit[0] -= 1.0

    fill()
    misc = pv_ref[PV_MISC:PV_MISC + 1, :]
    lane1 = lax.broadcasted_iota(jnp.int32, (1, LANE), 1)
    lo = lane1 < HEAD
    lo2 = (lax.broadcasted_iota(jnp.int32, (1, 2 * LANE), 1) % LANE) < HEAD
    dec_lanes = (lane1 >= ALPHA_L) & (lane1 < DT_L + B_HEADS)
    nega = jnp.where(dec_lanes, -jnp.exp(misc[:, M_ALOG:M_ALOG + LANE]), 0.0)
    r128 = lax.broadcasted_iota(jnp.int32, (LANE, LANE), 0)
    c128 = lax.broadcasted_iota(jnp.int32, (LANE, LANE), 1)
    same_head = (r128 // HEAD) == (c128 // HEAD)
    ones_head = jnp.where(same_head, 1.0, 0.0).astype(BF16)
    row_lo = r128 < HEAD

    def head_sum(v):
        return _dot_right_exact(v, ones_head, 1)

    first = SUBLANE - (CONV_W - 1)
    tiles_per_piece = MXU_TILE // LANE
    for j in range(CONV_CH // LANE):
        cs = slice(j * LANE, (j + 1) * LANE)
        for sq in range(bb):
            acc = cbuf[sq, first:first + tb, cs] * pv_ref[PV_CW:PV_CW + 1, cs]
            for k in range(1, CONV_W):
                acc = acc + cbuf[sq, first + k:first + k + tb, cs] * pv_ref[PV_CW + k:PV_CW + k + 1, cs]
            if j * LANE >= XBC_OFF:
                acc = acc + pv_ref[PV_CB:PV_CB + 1, cs]
            if j * LANE < XC_OFF:
                acc = _silu(acc)
            if j * LANE < V_OFF:
                acc = acc * lax.rsqrt(head_sum(acc * acc) + EPS)
                if j * LANE < K_OFF:
                    acc = acc * (A_DK ** -0.5)
            act[sq, :, cs] = acc
        if j % tiles_per_piece == tiles_per_piece - 1:
            ps_ = slice((j + 1 - tiles_per_piece) * LANE, (j + 1) * LANE)
            for sq in range(bb):
                tail = cbuf[sq, tb:tb + SUBLANE, ps_]
                co_ref[sq, :, ps_] = tail
                cbuf[sq, 0:SUBLANE, ps_] = tail
        fill()
    for j in range(SM_OFF // LANE):
        cs = slice(j * LANE, (j + 1) * LANE)
        for sq in range(bb):
            gv = gates[sq, :, cs]
            gact[sq, :, cs] = _silu(gv) if j * LANE < GC_OFF else _gelu_tanh(gv)
        fill()
    assert not fillers or len(fillers) >= n_piece
    for sq in range(bb):
        sm = gates[sq, :, SM_OFF:SM_OFF + LANE]
        sp = _softplus(sm + misc[:, M_BIAS:M_BIAS + LANE])
        scal[sq, 0] = _sigmoid(sm)
        scal[sq, 1] = sp
        scal[sq, 2] = sp * nega

    ri = lax.broadcasted_iota(jnp.int32, (c, 2 * c), 0)
    cjw = lax.broadcasted_iota(jnp.int32, (c, 2 * c), 1)
    first_w = cjw < c
    cj = cjw % c
    causal_w = ri >= cj
    strict_w = ri > cj
    tril = jnp.where(lax.broadcasted_iota(jnp.int32, (c, c), 0) >= lax.broadcasted_iota(jnp.int32, (c, c), 1),
                     1.0, 0.0).astype(BF16)

    items = [(sq, ck) for sq in range(bb) for ck in range(nck)]
    rows = {it: slice(it[1] * c, (it[1] + 1) * c) for it in items}

    sc = {}
    for it in items:
        sq, rs = it[0], rows[it]
        g = _dot_left_exact(tril, scal[sq, 2, rs, :], 2)
        gt2 = jnp.concatenate([g, g], axis=0).T
        glast = g[c - 1:c, :]
        sc[it] = dict(g=g, gt2=gt2, eg=jnp.exp(g), egl=jnp.exp(glast - g), gtot=jnp.exp(glast))
        fill()

    def pairvec(arr, la_, lb_):
        return jnp.where(lo, arr[:, la_:la_ + 1], arr[:, lb_:lb_ + 1])

    def decay_w(d, la_, lb_):
        gcol = jnp.where(first_w, d['g'][:, la_:la_ + 1], d['g'][:, lb_:lb_ + 1])
        grow = jnp.where(first_w[0:1], d['gt2'][la_:la_ + 1, :], d['gt2'][lb_:lb_ + 1, :])
        return jnp.exp(jnp.where(causal_w, gcol - grow, -jnp.inf))

    def state_scale(d, la_, lb_):
        return jnp.where(row_lo, d['gtot'][:, la_:la_ + 1], d['gtot'][:, lb_:lb_ + 1])

    units = [(it, p) for it in items for p in range(N_PAIR)]

    ga = {}
    for un in units:
        (sq, _), p = un
        rs = rows[un[0]]
        qn = act[sq, rs, p * LANE:(p + 1) * LANE]
        kn = act[sq, rs, K_OFF + p * LANE:K_OFF + (p + 1) * LANE]
        ksplit = _head_split_rows(kn, lo).astype(BF16)
        kq = _dot_nt(jnp.concatenate([kn.astype(BF16), qn.astype(BF16)], axis=0), ksplit)
        ga[un] = dict(qn=qn, kn=kn, kq=kq)
        fill()
    lws = []
    for un in units:
        (sq, _), p = un
        rs, d, a = rows[un[0]], sc[un[0]], ga[un]
        ha, hb_ = 2 * p, 2 * p + 1
        gam = decay_w(d, ALPHA_L + ha, ALPHA_L + hb_)
        beta_c = scal[sq, 0, rs, :]
        bcol = jnp.where(first_w, beta_c[:, BETA_L + ha:BETA_L + ha + 1], beta_c[:, BETA_L + hb_:BETA_L + hb_ + 1])
        lws.append(jnp.where(strict_w, bcol * a['kq'][:c] * gam, 0.0))
        a['qk'] = (a['kq'][c:] * gam).astype(BF16)
        bvec = pairvec(beta_c, BETA_L + ha, BETA_L + hb_)
        egv = pairvec(d['eg'], ALPHA_L + ha, ALPHA_L + hb_)
        eglv = pairvec(d['egl'], ALPHA_L + ha, ALPHA_L + hb_)
        vp = act[sq, rs, V_OFF + p * LANE:V_OFF + (p + 1) * LANE]
        rhs = jnp.concatenate([bvec * vp, bvec * egv * a['kn']], axis=1)
        a['rhs'] = _head_split_rows(rhs, lo2).astype(BF16)
        a['qd'] = a['qn'] * egv
        a['kd'] = (a['kn'] * eglv).astype(BF16)
        del a['kq']
        fill()

    gb = {}
    for it in items:
        sq, rs = it[0], rows[it]
        bm = act[sq, rs, BM_OFF:BM_OFF + LANE]
        cm = act[sq, rs, CM_OFF:CM_OFF + LANE]
        bm_sw = pltpu.roll(bm, HEAD, 1)
        cm_sw = pltpu.roll(cm, HEAD, 1)
        for p in range(N_PAIR):
            ha, hb_ = 2 * p, 2 * p + 1
            g0, g1 = ha // (B_HEADS // B_GROUPS), hb_ // (B_HEADS // B_GROUPS)
            if g0 == g1 == 0:
                bsel, csel = jnp.where(lo, bm, bm_sw), jnp.where(lo, cm, cm_sw)
            elif g0 == g1 == 1:
                bsel, csel = jnp.where(lo, bm_sw, bm), jnp.where(lo, cm_sw, cm)
            else:
                bsel, csel = bm, cm
            gb[(it, p)] = dict(bsel=bsel, csel=csel,
                               scores=_dot_nt(csel.astype(BF16), _head_split_rows(bsel, lo).astype(BF16)))
            fill()
    for un in units:
        (sq, _), p = un
        rs, d, b = rows[un[0]], sc[un[0]], gb[un]
        ha, hb_ = 2 * p, 2 * p + 1
        xs = act[sq, rs, XBC_OFF + p * LANE:XBC_OFF + (p + 1) * LANE]
        xdt = xs * pairvec(scal[sq, 1, rs, :], DT_L + ha, DT_L + hb_)
        b['xb'] = xdt.astype(BF16)
        mw = (b['scores'] * decay_w(d, DT_L + ha, DT_L + hb_)).astype(BF16)
        b['y'] = _dot(mw, _head_split_rows(xdt, lo).astype(BF16))
        b['cd'] = (b['csel'] * pairvec(d['eg'], DT_L + ha, DT_L + hb_)).astype(BF16)
        b['bd'] = (b['bsel'] * pairvec(d['egl'], DT_L + ha, DT_L + hb_)).astype(BF16)
        b['xs'] = xs
        del b['scores'], b['csel'], b['bsel']
        fill()

    tws = _tri_inv_wide(lws, c, fill)
    for un, tw in zip(units, tws):
        ga[un]['uw'] = _dot(tw.astype(BF16), ga[un]['rhs'])

    for ck in range(nck):
        cur = [((sq, ck), p) for sq in range(bb) for p in range(N_PAIR)]
        s_old = {un: s_ref[un[0][0], un[1]] for un in cur}
        h_old = {un: hs_ref[un[0][0], un[1]] for un in cur}
        wq = {un: _dot(jnp.concatenate([ga[un]['uw'][:, LANE:].astype(BF16), ga[un]['qd'].astype(BF16)], axis=0),
                       s_old[un].astype(BF16)) for un in cur}
        fill()
        yi = {un: _dot(gb[un]['cd'], h_old[un].astype(BF16)) for un in cur}
        fill()
        for un in cur:
            (sq, _), p = un
            rs, d, a, b = rows[un[0]], sc[un[0]], ga[un], gb[un]
            ha, hb_ = 2 * p, 2 * p + 1
            delta = a['uw'][:, :LANE] - wq[un][:c]
            o = wq[un][c:] + _dot(a['qk'], _head_split_rows(delta, lo).astype(BF16))
            upd = _dot_tn(a['kd'], delta.astype(BF16))
            s_ref[sq, p] = state_scale(d, ALPHA_L + ha, ALPHA_L + hb_) * s_old[un] + jnp.where(same_head, upd, 0.0)
            obuf[sq, rs, p * LANE:(p + 1) * LANE] = o
            updh = _dot_tn(b['bd'], b['xb'])
            hs_ref[sq, p] = state_scale(d, DT_L + ha, DT_L + hb_) * h_old[un] + jnp.where(same_head, updh, 0.0)
            y = b['y'] + yi[un] + misc[:, M_DSKIP + p * LANE:M_DSKIP + (p + 1) * LANE] * b['xs']
            obuf[sq, rs, A_WIDTH + p * LANE:A_WIDTH + (p + 1) * LANE] = (
                y * gact[sq, rs, ZB_OFF + p * LANE:ZB_OFF + (p + 1) * LANE])
            fill()

    g3r = lax.broadcasted_iota(jnp.int32, (B_WIDTH, B_WIDTH), 0) // (B_WIDTH // B_GROUPS)
    g3c = lax.broadcasted_iota(jnp.int32, (B_WIDTH, B_WIDTH), 1) // (B_WIDTH // B_GROUPS)
    ones_group = jnp.where(g3r == g3c, 1.0, 0.0).astype(BF16)
    rowt = lax.broadcasted_iota(jnp.int32, (tb, LANE), 0)
    for sq in range(bb):
        ms_ = slice(sq * tb, (sq + 1) * tb)
        for p in range(N_PAIR):
            o = obuf[sq, :, p * LANE:(p + 1) * LANE]
            ms = head_sum(o * o) * (1.0 / A_DV)
            on = o * lax.rsqrt(ms + EPS) * misc[:, M_NA:M_NA + LANE]
            mixo[ms_, p * LANE:(p + 1) * LANE] = (
                on * gact[sq, :, ZA_OFF + p * LANE:ZA_OFF + (p + 1) * LANE]).astype(BF16)
            fill()
        yb = obuf[sq, :, A_WIDTH:A_WIDTH + B_WIDTH]
        ms = _dot_right_exact(yb * yb, ones_group, 1) * (1.0 / (B_WIDTH // B_GROUPS))
        ob = yb * lax.rsqrt(ms + EPS) * misc[:, M_NB:M_NB + B_WIDTH]
        mixo[ms_, A_WIDTH:A_WIDTH + B_WIDTH] = ob.astype(BF16)
        fill()

        xc = act[sq, :, XC_OFF:XC_OFF + C_WIDTH]
        rig = _dot(xc.astype(BF16), wri_ref[...]) + misc[:, M_BRI:M_BRI + 2 * C_WIDTH]
        for half in range(C_WIDTH // LANE):
            hs_ = slice(half * LANE, (half + 1) * LANE)
            xch = xc[:, hs_]
            lam = misc[:, M_LAM + half * LANE:M_LAM + (half + 1) * LANE]
            log_a = -LRU_C * _sigmoid(rig[:, hs_]) * _softplus(-lam)
            a = jnp.exp(log_a)
            b = jnp.sqrt(1.0 - jnp.exp(2.0 * log_a)) * (
                _sigmoid(rig[:, C_WIDTH + half * LANE:C_WIDTH + (half + 1) * LANE]) * xch)
            d = 1
            while d < tb:
                keep = rowt >= d
                a_sh = jnp.where(keep, pltpu.roll(a, d, 0), 1.0)
                b_sh = jnp.where(keep, pltpu.roll(b, d, 0), 0.0)
                b = a * b_sh + b
                a = a * a_sh
                d *= 2
            hseq = a * l_ref[sq, :, hs_] + b
            l_ref[sq, :, hs_] = hseq[tb - 1:tb, :]
            mixo[ms_, A_WIDTH + B_WIDTH + half * LANE:A_WIDTH + B_WIDTH + (half + 1) * LANE] = (
                hseq * gact[sq, :, GC_OFF + half * LANE:GC_OFF + (half + 1) * LANE]).astype(BF16)
            fill()

    while fillers:
        fillers.pop(0)()
    mix = _dot(mixo[...], wout_ref[...])
    for sq in range(bb):
        xo_ref[sq] = x2c[sq] + mix[sq * tb:(sq + 1) * tb]


def _mixer(x, pv, win, wri, wout, n1, wg, wu, wd, s0, h0, l0, c0, *, chunk, tb, bb):
    bsz, l, _ = x.shape
    assert l % tb == 0 and tb % chunk == 0 and bsz % bb == 0
    n_t = l // tb
    seq = lambda b, t: (b, 0, 0)
    seq4 = lambda b, t: (b, 0, 0, 0)
    in_specs = [
        pl.BlockSpec((bb, tb, D_MODEL), lambda b, t: (b, t, 0)),
        pl.BlockSpec((bb, tb, D_MODEL), lambda b, t: (b, jnp.minimum(t + 1, n_t - 1), 0)),
        _resident((PV_ROWS, CONV_CH)),
        _resident((D_MODEL, D_IN_R)),
        _resident((C_WIDTH, 2 * C_WIDTH)),
        _resident((D_MIX, D_MODEL)),
        _resident((1, D_MODEL)),
        _resident((D_MODEL, D_FF)),
        _resident((D_MODEL, D_FF)),
        _resident((D_FF, D_MODEL)),
        pl.BlockSpec((bb, N_PAIR, LANE, LANE), seq4),
        pl.BlockSpec((bb, N_PAIR, LANE, LANE), seq4),
        pl.BlockSpec((bb, 1, C_WIDTH), seq),
        pl.BlockSpec((bb, SUBLANE, CONV_CH), seq),
    ]
    out_specs = [
        pl.BlockSpec((bb, tb, D_MODEL), lambda b, t: (b, t, 0)),
        pl.BlockSpec((bb, N_PAIR, LANE, LANE), seq4),
        pl.BlockSpec((bb, N_PAIR, LANE, LANE), seq4),
        pl.BlockSpec((bb, 1, C_WIDTH), seq),
        pl.BlockSpec((bb, SUBLANE, CONV_CH), seq),
    ]
    out_shape = [
        jax.ShapeDtypeStruct((bsz, l, D_MODEL), F32),
        jax.ShapeDtypeStruct((bsz, N_PAIR, LANE, LANE), F32),
        jax.ShapeDtypeStruct((bsz, N_PAIR, LANE, LANE), F32),
        jax.ShapeDtypeStruct((bsz, 1, C_WIDTH), F32),
        jax.ShapeDtypeStruct((bsz, SUBLANE, CONV_CH), F32),
    ]
    scratch = [
        pltpu.VMEM((bb, tb + SUBLANE, CONV_CH), F32),
        pltpu.VMEM((bb, tb, CONV_CH), F32),
        pltpu.VMEM((bb, tb, GATE_CH), F32),
        pltpu.VMEM((bb, tb, SM_OFF), F32),
        pltpu.VMEM((bb, 3, tb, LANE), F32),
        pltpu.VMEM((bb, tb, A_WIDTH + B_WIDTH), F32),
        pltpu.VMEM((bb * tb, D_MIX), BF16),
        pltpu.VMEM((bb * tb, D_MODEL), BF16),
        pltpu.VMEM((bb * tb, D_MODEL), BF16),
        pltpu.VMEM((bb * tb, MXU_TILE), F32),
        pltpu.VMEM((2, bb * tb, MXU_TILE), BF16),
        pltpu.VMEM((bb * tb, D_MODEL), F32),
        pltpu.VMEM((bb, tb, D_MODEL), F32),
        pltpu.VMEM((bb, tb, D_MODEL), F32),
    ]
    return pl.pallas_call(
        functools.partial(_mixer_kernel, chunk=chunk, tb=tb, bb=bb, n_t=n_t),
        out_shape=out_shape,
        grid=(bsz // bb, l // tb),
        in_specs=in_specs,
        out_specs=out_specs,
        scratch_shapes=scratch,
        compiler_params=pltpu.CompilerParams(dimension_semantics=("arbitrary", "arbitrary"),
                                             vmem_limit_bytes=VMEM_LIMIT),
        name="mixer_c%d" % chunk,
    )(x, x, pv, win, wri, wout, n1, wg, wu, wd, s0, h0, l0, c0)


def _pair_states(s):
    bsz = s.shape[0]
    s = s.reshape(bsz, N_PAIR, 2, HEAD, HEAD)
    z = jnp.zeros((bsz, N_PAIR, HEAD, HEAD), s.dtype)
    top = jnp.concatenate([s[:, :, 0], z], axis=-1)
    bot = jnp.concatenate([z, s[:, :, 1]], axis=-1)
    return jnp.concatenate([top, bot], axis=-2)


def _unpair_states(sp):
    a = sp[:, :, :HEAD, :HEAD]
    b = sp[:, :, HEAD:, HEAD:]
    return jnp.stack([a, b], axis=2).reshape(sp.shape[0], A_HEADS, HEAD, HEAD)


def _layer_params(lp):
    w = lp['w_in']
    o = 0
    cols = {}
    for name, size in (('qkv', A_CONV_CH), ('z_a', A_WIDTH), ('b_a', A_HEADS), ('a_a', A_HEADS), ('z_b', B_WIDTH),
                       ('xbc', B_CONV_CH), ('dt', B_HEADS), ('gate_c', C_WIDTH), ('x_c', C_WIDTH)):
        cols[name] = w[:, o:o + size]
        o += size
    pad = jnp.zeros((D_MODEL, LANE - 2 * A_HEADS - B_HEADS), w.dtype)
    win = jnp.concatenate([cols['qkv'], cols['xbc'], cols['x_c'], cols['z_a'], cols['z_b'], cols['gate_c'],
                           cols['b_a'], cols['a_a'], cols['dt'], pad], axis=1).astype(BF16)

    def lane_block(*pieces):
        v = jnp.zeros((LANE,), F32)
        for off, val in pieces:
            v = lax.dynamic_update_slice(v, val.astype(F32), (off,))
        return v

    misc = jnp.concatenate([
        lane_block((ALPHA_L, lp['dt_bias_a']), (DT_L, lp['dt_bias_b'])),
        lane_block((ALPHA_L, lp['a_log_a']), (DT_L, lp['a_log_b'])),
        jnp.tile(lp['norm_a_w'], 2),
        jnp.repeat(lp['d_skip_b'], B_HEADDIM),
        lp['norm_b_w'],
        lp['lru_lambda'],
        lp['b_rgate'], lp['b_igate'],
    ])
    misc = jnp.pad(misc, (0, CONV_CH - misc.shape[0]))
    cw = jnp.concatenate([lp['conv_a_w'], lp['conv_b_w'], lp['conv_c_w']], axis=1)
    cb = jnp.concatenate([jnp.zeros((A_CONV_CH,), F32), lp['conv_b_b'], lp['conv_c_b']])
    pv = jnp.concatenate([jnp.pad(lp['norm_mix'], (0, CONV_CH - D_MODEL))[None], cw, cb[None], misc[None],
                          jnp.zeros((1, CONV_CH), F32)], axis=0)

    def block_diag(wb):
        eye = jnp.eye(C_BLOCKS, dtype=wb.dtype)
        return jnp.einsum('ncd,nm->ncmd', wb, eye).reshape(C_WIDTH, C_WIDTH)

    wri = jnp.concatenate([block_diag(lp['w_rgate']), block_diag(lp['w_igate'])], axis=1).astype(BF16)
    return dict(
        pv=pv, win=win, wri=wri, wout=lp['w_out'].astype(BF16),
        n1=lp['norm_ffn1'][None], g1=lp['ffn1_w_gate'].astype(BF16), u1=lp['ffn1_w_up'].astype(BF16),
        d1=lp['ffn1_w_down'].astype(BF16),
        n2=lp['norm_ffn2'][None], g2=lp['ffn2_w_gate'].astype(BF16), u2=lp['ffn2_w_up'].astype(BF16),
        d2=lp['ffn2_w_down'].astype(BF16),
    )


def _trunk(x, states, layers, norm_final, *, chunk, tb, bb, tm):
    bsz, l, _ = x.shape
    delta_s, delta_conv, ssd_h, ssd_conv, lru_h, lru_conv = states
    outs = [[] for _ in range(6)]
    for layer in range(DEPTH):
        lw = layers[layer]
        conv0 = jnp.concatenate([delta_conv[layer], ssd_conv[layer], lru_conv[layer]], axis=-1).astype(F32)
        conv0 = jnp.pad(conv0, ((0, 0), (SUBLANE - (CONV_W - 1), 0), (0, 0)))
        x3, s_new, h_new, l_new, c_new = _mixer(
            x, lw['pv'], lw['win'], lw['wri'], lw['wout'], lw['n1'], lw['g1'], lw['u1'], lw['d1'],
            _pair_states(delta_s[layer].astype(F32)), _pair_states(ssd_h[layer].astype(F32)),
            lru_h[layer].astype(F32)[:, None, :], conv0, chunk=chunk, tb=tb, bb=bb)
        fin = norm_final[None] if layer == DEPTH - 1 else None
        x = _ffn(x3.reshape(bsz * l, D_MODEL), lw['n2'], lw['g2'], lw['u2'], lw['d2'], fin, tm=tm).reshape(bsz, l, D_MODEL)
        c_new = c_new[:, SUBLANE - (CONV_W - 1):, :]
        outs[0].append(_unpair_states(s_new))
        outs[1].append(c_new[..., :A_CONV_CH])
        outs[2].append(_unpair_states(h_new))
        outs[3].append(c_new[..., XBC_OFF:XBC_OFF + B_CONV_CH])
        outs[4].append(l_new[:, 0, :])
        outs[5].append(c_new[..., XC_OFF:])
    return x, tuple(jnp.stack(o) for o in outs)


def kernel(x_prompt, x_sample, state_delta_s, state_delta_conv, state_ssd_h, state_ssd_conv, state_lru_h, state_lru_conv, norm_ffn1, ffn1_w_gate, ffn1_w_up, ffn1_w_down, norm_mix, w_in, conv_a_w, a_log_a, dt_bias_a, norm_a_w, conv_b_w, conv_b_b, a_log_b, dt_bias_b, d_skip_b, norm_b_w, conv_c_w, conv_c_b, w_rgate, b_rgate, w_igate, b_igate, lru_lambda, w_out, norm_ffn2, ffn2_w_gate, ffn2_w_up, ffn2_w_down, norm_final):
    params = {
        'norm_ffn1': norm_ffn1, 'ffn1_w_gate': ffn1_w_gate, 'ffn1_w_up': ffn1_w_up, 'ffn1_w_down': ffn1_w_down,
        'norm_mix': norm_mix, 'w_in': w_in,
        'conv_a_w': conv_a_w, 'a_log_a': a_log_a, 'dt_bias_a': dt_bias_a, 'norm_a_w': norm_a_w,
        'conv_b_w': conv_b_w, 'conv_b_b': conv_b_b, 'a_log_b': a_log_b, 'dt_bias_b': dt_bias_b,
        'd_skip_b': d_skip_b, 'norm_b_w': norm_b_w,
        'conv_c_w': conv_c_w, 'conv_c_b': conv_c_b, 'w_rgate': w_rgate, 'b_rgate': b_rgate,
        'w_igate': w_igate, 'b_igate': b_igate, 'lru_lambda': lru_lambda,
        'w_out': w_out,
        'norm_ffn2': norm_ffn2, 'ffn2_w_gate': ffn2_w_gate, 'ffn2_w_up': ffn2_w_up, 'ffn2_w_down': ffn2_w_down,
    }
    layers = [_layer_params({k: v[i] for k, v in params.items()}) for i in range(DEPTH)]
    bp = x_prompt.shape[0]
    zero_states = (
        jnp.zeros((DEPTH, bp, A_HEADS, A_DK, A_DV), F32),
        jnp.zeros((DEPTH, bp, CONV_W - 1, A_CONV_CH), F32),
        jnp.zeros((DEPTH, bp, B_HEADS, B_STATE, B_HEADDIM), F32),
        jnp.zeros((DEPTH, bp, CONV_W - 1, B_CONV_CH), F32),
        jnp.zeros((DEPTH, bp, C_WIDTH), F32),
        jnp.zeros((DEPTH, bp, CONV_W - 1, C_WIDTH), F32),
    )
    y_prompt, p_states = _trunk(x_prompt, zero_states, layers, norm_final, chunk=128, tb=256, bb=1, tm=512)
    sample_states = (state_delta_s, state_delta_conv, state_ssd_h, state_ssd_conv, state_lru_h, state_lru_conv)
    y_sample, s_states = _trunk(x_sample, sample_states, layers, norm_final, chunk=64, tb=64, bb=4, tm=512)
    return (y_prompt, y_sample) + tuple(p_states) + tuple(s_states)
```

```python
import functools

import jax
import jax.numpy as jnp
from jax import lax
from jax.experimental import pallas as pl
from jax.experimental.pallas import tpu as pltpu

F32 = jnp.float32
BF16 = jnp.bfloat16

D_MODEL = 1024
DEPTH = 2
D_FF = 2816
EPS = 1e-6
CONV_W = 4
A_HEADS = 6
A_DK = 64
A_DV = 64
A_WIDTH = A_HEADS * A_DV
A_CONV_CH = 2 * A_HEADS * A_DK + A_WIDTH
B_HEADS = 6
B_HEADDIM = 64
B_WIDTH = B_HEADS * B_HEADDIM
B_GROUPS = 2
B_STATE = 64
B_CONV_CH = B_WIDTH + 2 * B_GROUPS * B_STATE
C_WIDTH = 256
C_BLOCKS = 8
C_BLOCK = C_WIDTH // C_BLOCKS
LRU_C = 8.0
D_MIX = A_WIDTH + B_WIDTH + C_WIDTH

LANE = 128
SUBLANE = 8
MXU_TILE = 256
assert D_FF % MXU_TILE == 0
HEAD = 64
N_PAIR = A_HEADS // 2
INV_BASE = 16

CONV_CH = A_CONV_CH + B_CONV_CH + C_WIDTH
K_OFF = A_HEADS * A_DK
V_OFF = 2 * A_HEADS * A_DK
XBC_OFF = A_CONV_CH
BM_OFF = XBC_OFF + B_WIDTH
CM_OFF = BM_OFF + B_GROUPS * B_STATE
XC_OFF = A_CONV_CH + B_CONV_CH
GATE_CH = A_WIDTH + B_WIDTH + C_WIDTH + LANE
ZA_OFF, ZB_OFF, GC_OFF, SM_OFF = 0, A_WIDTH, A_WIDTH + B_WIDTH, A_WIDTH + B_WIDTH + C_WIDTH
D_IN_R = CONV_CH + GATE_CH
BETA_L, ALPHA_L, DT_L = 0, A_HEADS, 2 * A_HEADS
PV_ROWS = 8
PV_NORM, PV_CW, PV_CB, PV_MISC = 0, 1, 5, 6
M_BIAS, M_ALOG, M_NA, M_DSKIP, M_NB, M_LAM, M_BRI = 0, 128, 256, 384, 768, 1152, 1408

VMEM_LIMIT = 56 * 1024 * 1024


def _dot(a, b):
    return jnp.dot(a, b, preferred_element_type=F32)


def _dot_nt(a, b):
    return lax.dot_general(a, b, (((1,), (1,)), ((), ())), preferred_element_type=F32)


def _dot_tn(a, b):
    return lax.dot_general(a, b, (((0,), (0,)), ((), ())), preferred_element_type=F32)


def _split(x, n):
    parts = []
    r = x
    for _ in range(n):
        p = r.astype(BF16)
        parts.append(p)
        r = r - p.astype(F32)
    return parts


def _dot_left_exact(m, x, n):
    return sum(_dot(m, p) for p in _split(x, n))


def _dot_right_exact(x, m, n):
    return sum(_dot(p, m) for p in _split(x, n))


def _sigmoid(x):
    return jax.nn.sigmoid(x)


def _silu(x):
    return x * _sigmoid(x)


def _softplus(x):
    return jnp.maximum(x, 0.0) + jnp.log1p(jnp.exp(-jnp.abs(x)))


def _gelu_tanh(x):
    return 0.5 * x * (1.0 + jnp.tanh(0.7978845608028654 * (x + 0.044715 * x * x * x)))


def _rms(x, w):
    ms = jnp.mean(x * x, axis=-1, keepdims=True)
    return x * lax.rsqrt(ms + EPS) * w


def _ffn_body(x_ref, nw_ref, wg_ref, wu_ref, wd_ref, fnw_ref, o_ref, n_split):
    x = x_ref[...]
    h = _rms(x, nw_ref[...]).astype(BF16)
    tiles = D_FF // MXU_TILE
    bounds = [MXU_TILE * ((tiles * j + n_split - 1) // n_split) for j in range(n_split)] + [D_FF]
    acc = None
    for lo_, hi_ in zip(bounds[:-1], bounds[1:]):
        g = _dot(h, wg_ref[:, lo_:hi_])
        u = _dot(h, wu_ref[:, lo_:hi_])
        a = (_silu(g) * u).astype(BF16)
        d = _dot(a, wd_ref[lo_:hi_, :])
        acc = d if acc is None else acc + d
    y = x + 0.5 * acc
    if fnw_ref is not None:
        y = _rms(y, fnw_ref[...])
    o_ref[...] = y


def _ffn_kernel(x_ref, nw_ref, wg_ref, wu_ref, wd_ref, o_ref, *, n_split):
    _ffn_body(x_ref, nw_ref, wg_ref, wu_ref, wd_ref, None, o_ref, n_split)


def _ffn_final_kernel(x_ref, nw_ref, wg_ref, wu_ref, wd_ref, fnw_ref, o_ref, *, n_split):
    _ffn_body(x_ref, nw_ref, wg_ref, wu_ref, wd_ref, fnw_ref, o_ref, n_split)


def _resident(shape):
    return pl.BlockSpec(shape, lambda *_: (0,) * len(shape), pipeline_mode=pl.Buffered(1))


def _ffn(x2d, nw, wg, wu, wd, final_w=None, *, tm):
    t = x2d.shape[0]
    assert t % tm == 0
    row = pl.BlockSpec((tm, D_MODEL), lambda i: (i, 0))
    in_specs = [row, _resident((1, D_MODEL)), _resident((D_MODEL, D_FF)), _resident((D_MODEL, D_FF)),
                _resident((D_FF, D_MODEL))]
    args = [x2d, nw, wg, wu, wd]
    if final_w is None:
        body = functools.partial(_ffn_kernel, n_split=2)
    else:
        body = functools.partial(_ffn_final_kernel, n_split=2)
        in_specs.append(_resident((1, D_MODEL)))
        args.append(final_w)
    return pl.pallas_call(
        body,
        out_shape=jax.ShapeDtypeStruct((t, D_MODEL), F32),
        grid=(t // tm,),
        in_specs=in_specs,
        out_specs=row,
        compiler_params=pltpu.CompilerParams(dimension_semantics=("arbitrary",), vmem_limit_bytes=VMEM_LIMIT),
        name="ffn_final" if final_w is not None else "ffn",
    )(*args)


def _wide_blockdiag(xw, c):
    if c % LANE == 0:
        z = jnp.zeros((c, c), BF16)
        xb = xw.astype(BF16)
        return jnp.concatenate([jnp.concatenate([xb[:, :c], z], axis=1),
                                jnp.concatenate([z, xb[:, c:]], axis=1)], axis=0)
    first = lax.broadcasted_iota(jnp.int32, (1, 2 * c), 1) < c
    return jnp.concatenate([jnp.where(first, xw, 0.0).astype(BF16), jnp.where(first, 0.0, xw).astype(BF16)], axis=0)


def _head_split_rows(x, lo):
    return jnp.concatenate([jnp.where(lo, x, 0.0), jnp.where(lo, 0.0, x)], axis=0)


def _tri_inv_wide(lws, c, between):
    ri = lax.broadcasted_iota(jnp.int32, (c, 2 * c), 0)
    cj = lax.broadcasted_iota(jnp.int32, (c, 2 * c), 1) % c
    diag_blk = ri // INV_BASE == cj // INV_BASE
    eye = jnp.where(ri == cj, 1.0, 0.0)

    def mm(a, bw):
        return _dot(a.astype(BF16), _wide_blockdiag(bw, c))

    ns = [jnp.where(diag_blk, -lw, 0.0) for lw in lws]
    ts = [eye + n for n in ns]
    ps = [mm(n, n) for n in ns]
    between()
    n_pow = 2
    while 2 * n_pow < INV_BASE:
        outs = [_dot(jnp.concatenate([p.astype(BF16), t.astype(BF16)], axis=0),
                     _wide_blockdiag(p, c)) for p, t in zip(ps, ts)]
        between()
        ts = [t + o[c:] for t, o in zip(ts, outs)]
        ps = [o[:c] for o in outs]
        n_pow *= 2
    ts = [t + mm(t, p) for t, p in zip(ts, ps)]
    between()
    s = INV_BASE
    while s < c:
        sub = (ri // (2 * s) == cj // (2 * s)) & ((ri // s) % 2 == 1) & ((cj // s) % 2 == 0)
        ets = [mm(jnp.where(sub, lw, 0.0), t) for lw, t in zip(lws, ts)]
        between()
        ts = [t - mm(t, et) for t, et in zip(ts, ets)]
        between()
        s *= 2
    return ts


def _mixer_kernel(x_ref, xn_ref, pv_ref, win_ref, wri_ref, wout_ref, n1_ref, wg_ref, wu_ref, wd_ref,
                  s0_ref, h0_ref, l0_ref, c0_ref,
                  xo_ref, s_ref, hs_ref, l_ref, co_ref,
                  cbuf, act, gates, gact, scal, obuf, mixo, hbuf, hbuf1, gbuf, abuf, facc, x2c, x2n,
                  *, chunk, tb, bb, n_t):
    c = chunk
    nck = tb // c
    t_idx = pl.program_id(1)
    n_piece = pl.cdiv(D_IN_R, MXU_TILE)
    conv_pieces = CONV_CH // MXU_TILE
    ff_tiles = D_FF // MXU_TILE

    def front_pieces(src_ref, x2dst):
        def norm1():
            for sq in range(bb):
                hbuf1[sq * tb:(sq + 1) * tb, :] = _rms(src_ref[sq], n1_ref[...]).astype(BF16)

        def gate(j):
            gbuf[...] = _dot(hbuf1[...], wg_ref[:, j * MXU_TILE:(j + 1) * MXU_TILE])

        def up(j):
            u = _dot(hbuf1[...], wu_ref[:, j * MXU_TILE:(j + 1) * MXU_TILE])
            abuf[j % 2] = (_silu(gbuf[...]) * u).astype(BF16)

        def down(j):
            d = _dot(abuf[j % 2], wd_ref[j * MXU_TILE:(j + 1) * MXU_TILE, :])
            facc[...] = d if j == 0 else facc[...] + d

        def norm2():
            for sq in range(bb):
                x2 = src_ref[sq] + 0.5 * facc[sq * tb:(sq + 1) * tb, :]
                x2dst[sq] = x2
                hbuf[sq * tb:(sq + 1) * tb, :] = _rms(x2, pv_ref[PV_NORM:PV_NORM + 1, 0:D_MODEL]).astype(BF16)

        def piece(j):
            lo_, hi_ = j * MXU_TILE, min((j + 1) * MXU_TILE, D_IN_R)
            res = _dot(hbuf[...], win_ref[:, lo_:hi_])
            for sq in range(bb):
                part = res[sq * tb:(sq + 1) * tb]
                if j < conv_pieces:
                    cbuf[sq, SUBLANE:SUBLANE + tb, lo_:hi_] = part
                else:
                    gates[sq, :, lo_ - CONV_CH:hi_ - CONV_CH] = part

        thunks = [norm1]
        for j in range(ff_tiles):
            thunks += [functools.partial(gate, j), functools.partial(up, j)]
            if j > 0:
                thunks.append(functools.partial(down, j - 1))
        thunks += [functools.partial(down, ff_tiles - 1), norm2]
        return thunks + [functools.partial(piece, j) for j in range(n_piece)]

    @pl.when(t_idx == 0)
    def _():
        s_ref[...] = s0_ref[...]
        hs_ref[...] = h0_ref[...]
        l_ref[...] = l0_ref[...]
        cbuf[:, 0:SUBLANE, :] = c0_ref[...]
        for thunk in front_pieces(x_ref, x2c):
            thunk()

    if n_t > 1:
        @pl.when(t_idx > 0)
        def _():
            x2c[...] = x2n[...]

    fillers = front_pieces(xn_ref, x2n) if n_t > 1 else []
    n_units = bb * nck * N_PAIR
    n_hooks = (1 + CONV_CH // LANE + SM_OFF // LANE + bb * nck + 4 * n_units
               + 4 + 2 * max(0, (c // INV_BASE).bit_length() - 1) + nck * (2 + bb * N_PAIR) + 6 * bb)
    per_hook = len(fillers) / (n_hooks - 4)
    credit = [0.0]

    def fill():
        credit[0] += per_hook
        while credit[0] >= 1.0 and fillers:
            fillers.pop(0)()
            credit[0] -= 1.0

    fill()
    misc = pv_ref[PV_MISC:PV_MISC + 1, :]
    lane1 = lax.broadcasted_iota(jnp.int32, (1, LANE), 1)
    lo = lane1 < HEAD
    lo2 = (lax.broadcasted_iota(jnp.int32, (1, 2 * LANE), 1) % LANE) < HEAD
    dec_lanes = (lane1 >= ALPHA_L) & (lane1 < DT_L + B_HEADS)
    nega = jnp.where(dec_lanes, -jnp.exp(misc[:, M_ALOG:M_ALOG + LANE]), 0.0)
    r128 = lax.broadcasted_iota(jnp.int32, (LANE, LANE), 0)
    c128 = lax.broadcasted_iota(jnp.int32, (LANE, LANE), 1)
    same_head = (r128 // HEAD) == (c128 // HEAD)
    ones_head = jnp.where(same_head, 1.0, 0.0).astype(BF16)
    row_lo = r128 < HEAD

    def head_sum(v):
        return _dot_right_exact(v, ones_head, 1)

    first = SUBLANE - (CONV_W - 1)
    tiles_per_piece = MXU_TILE // LANE
    for j in range(CONV_CH // LANE):
        cs = slice(j * LANE, (j + 1) * LANE)
        for sq in range(bb):
            acc = cbuf[sq, first:first + tb, cs] * pv_ref[PV_CW:PV_CW + 1, cs]
            for k in range(1, CONV_W):
                acc = acc + cbuf[sq, first + k:first + k + tb, cs] * pv_ref[PV_CW + k:PV_CW + k + 1, cs]
            if j * LANE >= XBC_OFF:
                acc = acc + pv_ref[PV_CB:PV_CB + 1, cs]
            if j * LANE < XC_OFF:
                acc = _silu(acc)
            if j * LANE < V_OFF:
                acc = acc * lax.rsqrt(head_sum(acc * acc) + EPS)
                if j * LANE < K_OFF:
                    acc = acc * (A_DK ** -0.5)
            act[sq, :, cs] = acc
        if j % tiles_per_piece == tiles_per_piece - 1:
            ps_ = slice((j + 1 - tiles_per_piece) * LANE, (j + 1) * LANE)
            for sq in range(bb):
                tail = cbuf[sq, tb:tb + SUBLANE, ps_]
                co_ref[sq, :, ps_] = tail
                cbuf[sq, 0:SUBLANE, ps_] = tail
        fill()
    for j in range(SM_OFF // LANE):
        cs = slice(j * LANE, (j + 1) * LANE)
        for sq in range(bb):
            gv = gates[sq, :, cs]
            gact[sq, :, cs] = _silu(gv) if j * LANE < GC_OFF else _gelu_tanh(gv)
        fill()
    assert not fillers or len(fillers) >= n_piece
    for sq in range(bb):
        sm = gates[sq, :, SM_OFF:SM_OFF + LANE]
        sp = _softplus(sm + misc[:, M_BIAS:M_BIAS + LANE])
        scal[sq, 0] = _sigmoid(sm)
        scal[sq, 1] = sp
        scal[sq, 2] = sp * nega

    ri = lax.broadcasted_iota(jnp.int32, (c, 2 * c), 0)
    cjw = lax.broadcasted_iota(jnp.int32, (c, 2 * c), 1)
    first_w = cjw < c
    cj = cjw % c
    causal_w = ri >= cj
    strict_w = ri > cj
    tril = jnp.where(lax.broadcasted_iota(jnp.int32, (c, c), 0) >= lax.broadcasted_iota(jnp.int32, (c, c), 1),
                     1.0, 0.0).astype(BF16)

    items = [(sq, ck) for sq in range(bb) for ck in range(nck)]
    rows = {it: slice(it[1] * c, (it[1] + 1) * c) for it in items}

    sc = {}
    for it in items:
        sq, rs = it[0], rows[it]
        g = _dot_left_exact(tril, scal[sq, 2, rs, :], 2)
        gt2 = jnp.concatenate([g, g], axis=0).T
        glast = g[c - 1:c, :]
        sc[it] = dict(g=g, gt2=gt2, eg=jnp.exp(g), egl=jnp.exp(glast - g), gtot=jnp.exp(glast))
        fill()

    def pairvec(arr, la_, lb_):
        return jnp.where(lo, arr[:, la_:la_ + 1], arr[:, lb_:lb_ + 1])

    def decay_w(d, la_, lb_):
        gcol = jnp.where(first_w, d['g'][:, la_:la_ + 1], d['g'][:, lb_:lb_ + 1])
        grow = jnp.where(first_w[0:1], d['gt2'][la_:la_ + 1, :], d['gt2'][lb_:lb_ + 1, :])
        return jnp.exp(jnp.where(causal_w, gcol - grow, -jnp.inf))

    def state_scale(d, la_, lb_):
        return jnp.where(row_lo, d['gtot'][:, la_:la_ + 1], d['gtot'][:, lb_:lb_ + 1])

    units = [(it, p) for it in items for p in range(N_PAIR)]

    ga = {}
    for un in units:
        (sq, _), p = un
        rs = rows[un[0]]
        qn = act[sq, rs, p * LANE:(p + 1) * LANE]
        kn = act[sq, rs, K_OFF + p * LANE:K_OFF + (p + 1) * LANE]
        ksplit = _head_split_rows(kn, lo).astype(BF16)
        kq = _dot_nt(jnp.concatenate([kn.astype(BF16), qn.astype(BF16)], axis=0), ksplit)
        ga[un] = dict(qn=qn, kn=kn, kq=kq)
        fill()
    lws = []
    for un in units:
        (sq, _), p = un
        rs, d, a = rows[un[0]], sc[un[0]], ga[un]
        ha, hb_ = 2 * p, 2 * p + 1
        gam = decay_w(d, ALPHA_L + ha, ALPHA_L + hb_)
        beta_c = scal[sq, 0, rs, :]
        bcol = jnp.where(first_w, beta_c[:, BETA_L + ha:BETA_L + ha + 1], beta_c[:, BETA_L + hb_:BETA_L + hb_ + 1])
        lws.append(jnp.where(strict_w, bcol * a['kq'][:c] * gam, 0.0))
        a['qk'] = (a['kq'][c:] * gam).astype(BF16)
        bvec = pairvec(beta_c, BETA_L + ha, BETA_L + hb_)
        egv = pairvec(d['eg'], ALPHA_L + ha, ALPHA_L + hb_)
        eglv = pairvec(d['egl'], ALPHA_L + ha, ALPHA_L + hb_)
        vp = act[sq, rs, V_OFF + p * LANE:V_OFF + (p + 1) * LANE]
        rhs = jnp.concatenate([bvec * vp, bvec * egv * a['kn']], axis=1)
        a['rhs'] = _head_split_rows(rhs, lo2).astype(BF16)
        a['qd'] = a['qn'] * egv
        a['kd'] = (a['kn'] * eglv).astype(BF16)
        del a['kq']
        fill()

    gb = {}
    for it in items:
        sq, rs = it[0], rows[it]
        bm = act[sq, rs, BM_OFF:BM_OFF + LANE]
        cm = act[sq, rs, CM_OFF:CM_OFF + LANE]
        bm_sw = pltpu.roll(bm, HEAD, 1)
        cm_sw = pltpu.roll(cm, HEAD, 1)
        for p in range(N_PAIR):
            ha, hb_ = 2 * p, 2 * p + 1
            g0, g1 = ha // (B_HEADS // B_GROUPS), hb_ // (B_HEADS // B_GROUPS)
            if g0 == g1 == 0:
                bsel, csel = jnp.where(lo, bm, bm_sw), jnp.where(lo, cm, cm_sw)
            elif g0 == g1 == 1:
                bsel, csel = jnp.where(lo, bm_sw, bm), jnp.where(lo, cm_sw, cm)
            else:
                bsel, csel = bm, cm
            gb[(it, p)] = dict(bsel=bsel, csel=csel,
                               scores=_dot_nt(csel.astype(BF16), _head_split_rows(bsel, lo).astype(BF16)))
            fill()
    for un in units:
        (sq, _), p = un
        rs, d, b = rows[un[0]], sc[un[0]], gb[un]
        ha, hb_ = 2 * p, 2 * p + 1
        xs = act[sq, rs, XBC_OFF + p * LANE:XBC_OFF + (p + 1) * LANE]
        xdt = xs * pairvec(scal[sq, 1, rs, :], DT_L + ha, DT_L + hb_)
        b['xb'] = xdt.astype(BF16)
        mw = (b['scores'] * decay_w(d, DT_L + ha, DT_L + hb_)).astype(BF16)
        b['y'] = _dot(mw, _head_split_rows(xdt, lo).astype(BF16))
        b['cd'] = (b['csel'] * pairvec(d['eg'], DT_L + ha, DT_L + hb_)).astype(BF16)
        b['bd'] = (b['bsel'] * pairvec(d['egl'], DT_L + ha, DT_L + hb_)).astype(BF16)
        b['xs'] = xs
        del b['scores'], b['csel'], b['bsel']
        fill()

    tws = _tri_inv_wide(lws, c, fill)
    for un, tw in zip(units, tws):
        ga[un]['uw'] = _dot(tw.astype(BF16), ga[un]['rhs'])

    for ck in range(nck):
        cur = [((sq, ck), p) for sq in range(bb) for p in range(N_PAIR)]
        s_old = {un: s_ref[un[0][0], un[1]] for un in cur}
        h_old = {un: hs_ref[un[0][0], un[1]] for un in cur}
        wq = {un: _dot(jnp.concatenate([ga[un]['uw'][:, LANE:].astype(BF16), ga[un]['qd'].astype(BF16)], axis=0),
                       s_old[un].astype(BF16)) for un in cur}
        fill()
        yi = {un: _dot(gb[un]['cd'], h_old[un].astype(BF16)) for un in cur}
        fill()
        for un in cur:
            (sq, _), p = un
            rs, d, a, b = rows[un[0]], sc[un[0]], ga[un], gb[un]
            ha, hb_ = 2 * p, 2 * p + 1
            delta = a['uw'][:, :LANE] - wq[un][:c]
            o = wq[un][c:] + _dot(a['qk'], _head_split_rows(delta, lo).astype(BF16))
            upd = _dot_tn(a['kd'], delta.astype(BF16))
            s_ref[sq, p] = state_scale(d, ALPHA_L + ha, ALPHA_L + hb_) * s_old[un] + jnp.where(same_head, upd, 0.0)
            obuf[sq, rs, p * LANE:(p + 1) * LANE] = o
            updh = _dot_tn(b['bd'], b['xb'])
            hs_ref[sq, p] = state_scale(d, DT_L + ha, DT_L + hb_) * h_old[un] + jnp.where(same_head, updh, 0.0)
            y = b['y'] + yi[un] + misc[:, M_DSKIP + p * LANE:M_DSKIP + (p + 1) * LANE] * b['xs']
            obuf[sq, rs, A_WIDTH + p * LANE:A_WIDTH + (p + 1) * LANE] = (
                y * gact[sq, rs, ZB_OFF + p * LANE:ZB_OFF + (p + 1) * LANE])
            fill()

    g3r = lax.broadcasted_iota(jnp.int32, (B_WIDTH, B_WIDTH), 0) // (B_WIDTH // B_GROUPS)
    g3c = lax.broadcasted_iota(jnp.int32, (B_WIDTH, B_WIDTH), 1) // (B_WIDTH // B_GROUPS)
    ones_group = jnp.where(g3r == g3c, 1.0, 0.0).astype(BF16)
    rowt = lax.broadcasted_iota(jnp.int32, (tb, LANE), 0)
    for sq in range(bb):
        ms_ = slice(sq * tb, (sq + 1) * tb)
        for p in range(N_PAIR):
            o = obuf[sq, :, p * LANE:(p + 1) * LANE]
            ms = head_sum(o * o) * (1.0 / A_DV)
            on = o * lax.rsqrt(ms + EPS) * misc[:, M_NA:M_NA + LANE]
            mixo[ms_, p * LANE:(p + 1) * LANE] = (
                on * gact[sq, :, ZA_OFF + p * LANE:ZA_OFF + (p + 1) * LANE]).astype(BF16)
            fill()
        yb = obuf[sq, :, A_WIDTH:A_WIDTH + B_WIDTH]
        ms = _dot_right_exact(yb * yb, ones_group, 1) * (1.0 / (B_WIDTH // B_GROUPS))
        ob = yb * lax.rsqrt(ms + EPS) * misc[:, M_NB:M_NB + B_WIDTH]
        mixo[ms_, A_WIDTH:A_WIDTH + B_WIDTH] = ob.astype(BF16)
        fill()

        xc = act[sq, :, XC_OFF:XC_OFF + C_WIDTH]
        rig = _dot(xc.astype(BF16), wri_ref[...]) + misc[:, M_BRI:M_BRI + 2 * C_WIDTH]
        for half in range(C_WIDTH // LANE):
            hs_ = slice(half * LANE, (half + 1) * LANE)
            xch = xc[:, hs_]
            lam = misc[:, M_LAM + half * LANE:M_LAM + (half + 1) * LANE]
            log_a = -LRU_C * _sigmoid(rig[:, hs_]) * _softplus(-lam)
            a = jnp.exp(log_a)
            b = jnp.sqrt(1.0 - jnp.exp(2.0 * log_a)) * (
                _sigmoid(rig[:, C_WIDTH + half * LANE:C_WIDTH + (half + 1) * LANE]) * xch)
            d = 1
            while d < tb:
                keep = rowt >= d
                a_sh = jnp.where(keep, pltpu.roll(a, d, 0), 1.0)
                b_sh = jnp.where(keep, pltpu.roll(b, d, 0), 0.0)
                b = a * b_sh + b
                a = a * a_sh
                d *= 2
            hseq = a * l_ref[sq, :, hs_] + b
            l_ref[sq, :, hs_] = hseq[tb - 1:tb, :]
            mixo[ms_, A_WIDTH + B_WIDTH + half * LANE:A_WIDTH + B_WIDTH + (half + 1) * LANE] = (
                hseq * gact[sq, :, GC_OFF + half * LANE:GC_OFF + (half + 1) * LANE]).astype(BF16)
            fill()

    while fillers:
        fillers.pop(0)()
    mix = _dot(mixo[...], wout_ref[...])
    for sq in range(bb):
        xo_ref[sq] = x2c[sq] + mix[sq * tb:(sq + 1) * tb]


def _mixer(x, pv, win, wri, wout, n1, wg, wu, wd, s0, h0, l0, c0, *, chunk, tb, bb):
    bsz, l, _ = x.shape
    assert l % tb == 0 and tb % chunk == 0 and bsz % bb == 0
    n_t = l // tb
    seq = lambda b, t: (b, 0, 0)
    seq4 = lambda b, t: (b, 0, 0, 0)
    in_specs = [
        pl.BlockSpec((bb, tb, D_MODEL), lambda b, t: (b, t, 0)),
        pl.BlockSpec((bb, tb, D_MODEL), lambda b, t: (b, jnp.minimum(t + 1, n_t - 1), 0)),
        _resident((PV_ROWS, CONV_CH)),
        _resident((D_MODEL, D_IN_R)),
        _resident((C_WIDTH, 2 * C_WIDTH)),
        _resident((D_MIX, D_MODEL)),
        _resident((1, D_MODEL)),
        _resident((D_MODEL, D_FF)),
        _resident((D_MODEL, D_FF)),
        _resident((D_FF, D_MODEL)),
        pl.BlockSpec((bb, N_PAIR, LANE, LANE), seq4),
        pl.BlockSpec((bb, N_PAIR, LANE, LANE), seq4),
        pl.BlockSpec((bb, 1, C_WIDTH), seq),
        pl.BlockSpec((bb, SUBLANE, CONV_CH), seq),
    ]
    out_specs = [
        pl.BlockSpec((bb, tb, D_MODEL), lambda b, t: (b, t, 0)),
        pl.BlockSpec((bb, N_PAIR, LANE, LANE), seq4),
        pl.BlockSpec((bb, N_PAIR, LANE, LANE), seq4),
        pl.BlockSpec((bb, 1, C_WIDTH), seq),
        pl.BlockSpec((bb, SUBLANE, CONV_CH), seq),
    ]
    out_shape = [
        jax.ShapeDtypeStruct((bsz, l, D_MODEL), F32),
        jax.ShapeDtypeStruct((bsz, N_PAIR, LANE, LANE), F32),
        jax.ShapeDtypeStruct((bsz, N_PAIR, LANE, LANE), F32),
        jax.ShapeDtypeStruct((bsz, 1, C_WIDTH), F32),
        jax.ShapeDtypeStruct((bsz, SUBLANE, CONV_CH), F32),
    ]
    scratch = [
        pltpu.VMEM((bb, tb + SUBLANE, CONV_CH), F32),
        pltpu.VMEM((bb, tb, CONV_CH), F32),
        pltpu.VMEM((bb, tb, GATE_CH), F32),
        pltpu.VMEM((bb, tb, SM_OFF), F32),
        pltpu.VMEM((bb, 3, tb, LANE), F32),
        pltpu.VMEM((bb, tb, A_WIDTH + B_WIDTH), F32),
        pltpu.VMEM((bb * tb, D_MIX), BF16),
        pltpu.VMEM((bb * tb, D_MODEL), BF16),
        pltpu.VMEM((bb * tb, D_MODEL), BF16),
        pltpu.VMEM((bb * tb, MXU_TILE), F32),
        pltpu.VMEM((2, bb * tb, MXU_TILE), BF16),
        pltpu.VMEM((bb * tb, D_MODEL), F32),
        pltpu.VMEM((bb, tb, D_MODEL), F32),
        pltpu.VMEM((bb, tb, D_MODEL), F32),
    ]
    return pl.pallas_call(
        functools.partial(_mixer_kernel, chunk=chunk, tb=tb, bb=bb, n_t=n_t),
        out_shape=out_shape,
        grid=(bsz // bb, l // tb),
        in_specs=in_specs,
        out_specs=out_specs,
        scratch_shapes=scratch,
        compiler_params=pltpu.CompilerParams(dimension_semantics=("arbitrary", "arbitrary"),
                                             vmem_limit_bytes=VMEM_LIMIT),
        name="mixer_c%d" % chunk,
    )(x, x, pv, win, wri, wout, n1, wg, wu, wd, s0, h0, l0, c0)


def _pair_states(s):
    bsz = s.shape[0]
    s = s.reshape(bsz, N_PAIR, 2, HEAD, HEAD)
    z = jnp.zeros((bsz, N_PAIR, HEAD, HEAD), s.dtype)
    top = jnp.concatenate([s[:, :, 0], z], axis=-1)
    bot = jnp.concatenate([z, s[:, :, 1]], axis=-1)
    return jnp.concatenate([top, bot], axis=-2)


def _unpair_states(sp):
    a = sp[:, :, :HEAD, :HEAD]
    b = sp[:, :, HEAD:, HEAD:]
    return jnp.stack([a, b], axis=2).reshape(sp.shape[0], A_HEADS, HEAD, HEAD)


W_ZA = A_CONV_CH
W_BA = W_ZA + A_WIDTH
W_AA = W_BA + A_HEADS
W_ZB = W_AA + A_HEADS
W_XBC = W_ZB + B_WIDTH
W_DT = W_XBC + B_CONV_CH
W_GC = W_DT + B_HEADS
W_XC = W_GC + C_WIDTH
D_IN = W_XC + C_WIDTH
assert W_BA % LANE == BETA_L and W_AA % LANE == ALPHA_L and W_DT % LANE == DT_L


def _reorder_kernel(w_ref, o_ref):
    def put(dst, src, size):
        o_ref[:, dst:dst + size] = w_ref[0, :, src:src + size].astype(BF16)

    put(0, 0, A_CONV_CH)
    put(XBC_OFF, W_XBC, B_CONV_CH)
    put(XC_OFF, W_XC, C_WIDTH)
    put(CONV_CH + ZA_OFF, W_ZA, A_WIDTH)
    put(CONV_CH + ZB_OFF, W_ZB, B_WIDTH)
    put(CONV_CH + GC_OFF, W_GC, C_WIDTH)
    lane = lax.broadcasted_iota(jnp.int32, (1, LANE), 1)
    ba = w_ref[0, :, W_BA - BETA_L:W_BA - BETA_L + LANE]
    dt = w_ref[0, :, W_DT - DT_L:W_DT - DT_L + LANE]
    small = jnp.where(lane < DT_L, ba, jnp.where(lane < DT_L + B_HEADS, dt, 0.0))
    o_ref[:, CONV_CH + SM_OFF:CONV_CH + SM_OFF + LANE] = small.astype(BF16)


def _reorder_w_in(w_in, layer, rows=256):
    return pl.pallas_call(
        _reorder_kernel,
        out_shape=jax.ShapeDtypeStruct((D_MODEL, D_IN_R), BF16),
        grid=(D_MODEL // rows,),
        in_specs=[pl.BlockSpec((1, rows, D_IN), lambda i: (layer, i, 0))],
        out_specs=pl.BlockSpec((rows, D_IN_R), lambda i: (i, 0)),
        name="reorder_w_in",
    )(w_in)


def _layer_params(lp, w_in, layer):
    win = _reorder_w_in(w_in, layer)

    def lane_block(*pieces):
        v = jnp.zeros((LANE,), F32)
        for off, val in pieces:
            v = lax.dynamic_update_slice(v, val.astype(F32), (off,))
        return v

    misc = jnp.concatenate([
        lane_block((ALPHA_L, lp['dt_bias_a']), (DT_L, lp['dt_bias_b'])),
        lane_block((ALPHA_L, lp['a_log_a']), (DT_L, lp['a_log_b'])),
        jnp.tile(lp['norm_a_w'], 2),
        jnp.repeat(lp['d_skip_b'], B_HEADDIM),
        lp['norm_b_w'],
        lp['lru_lambda'],
        lp['b_rgate'], lp['b_igate'],
    ])
    misc = jnp.pad(misc, (0, CONV_CH - misc.shape[0]))
    cw = jnp.concatenate([lp['conv_a_w'], lp['conv_b_w'], lp['conv_c_w']], axis=1)
    cb = jnp.concatenate([jnp.zeros((A_CONV_CH,), F32), lp['conv_b_b'], lp['conv_c_b']])
    pv = jnp.concatenate([jnp.pad(lp['norm_mix'], (0, CONV_CH - D_MODEL))[None], cw, cb[None], misc[None],
                          jnp.zeros((1, CONV_CH), F32)], axis=0)

    def block_diag(wb):
        eye = jnp.eye(C_BLOCKS, dtype=wb.dtype)
        return jnp.einsum('ncd,nm->ncmd', wb, eye).reshape(C_WIDTH, C_WIDTH)

    wri = jnp.concatenate([block_diag(lp['w_rgate']), block_diag(lp['w_igate'])], axis=1).astype(BF16)
    return dict(
        pv=pv, win=win, wri=wri, wout=lp['w_out'].astype(BF16),
        n1=lp['norm_ffn1'][None], g1=lp['ffn1_w_gate'].astype(BF16), u1=lp['ffn1_w_up'].astype(BF16),
        d1=lp['ffn1_w_down'].astype(BF16),
        n2=lp['norm_ffn2'][None], g2=lp['ffn2_w_gate'].astype(BF16), u2=lp['ffn2_w_up'].astype(BF16),
        d2=lp['ffn2_w_down'].astype(BF16),
    )


def _trunk(x, states, layers, norm_final, *, chunk, tb, bb, tm):
    bsz, l, _ = x.shape
    delta_s, delta_conv, ssd_h, ssd_conv, lru_h, lru_conv = states
    outs = [[] for _ in range(6)]
    for layer in range(DEPTH):
        lw = layers[layer]
        conv0 = jnp.concatenate([delta_conv[layer], ssd_conv[layer], lru_conv[layer]], axis=-1).astype(F32)
        conv0 = jnp.pad(conv0, ((0, 0), (SUBLANE - (CONV_W - 1), 0), (0, 0)))
        x3, s_new, h_new, l_new, c_new = _mixer(
            x, lw['pv'], lw['win'], lw['wri'], lw['wout'], lw['n1'], lw['g1'], lw['u1'], lw['d1'],
            _pair_states(delta_s[layer].astype(F32)), _pair_states(ssd_h[layer].astype(F32)),
            lru_h[layer].astype(F32)[:, None, :], conv0, chunk=chunk, tb=tb, bb=bb)
        fin = norm_final[None] if layer == DEPTH - 1 else None
        x = _ffn(x3.reshape(bsz * l, D_MODEL), lw['n2'], lw['g2'], lw['u2'], lw['d2'], fin, tm=tm).reshape(bsz, l, D_MODEL)
        c_new = c_new[:, SUBLANE - (CONV_W - 1):, :]
        outs[0].append(_unpair_states(s_new))
        outs[1].append(c_new[..., :A_CONV_CH])
        outs[2].append(_unpair_states(h_new))
        outs[3].append(c_new[..., XBC_OFF:XBC_OFF + B_CONV_CH])
        outs[4].append(l_new[:, 0, :])
        outs[5].append(c_new[..., XC_OFF:])
    return x, tuple(jnp.stack(o) for o in outs)


def kernel(x_prompt, x_sample, state_delta_s, state_delta_conv, state_ssd_h, state_ssd_conv, state_lru_h, state_lru_conv, norm_ffn1, ffn1_w_gate, ffn1_w_up, ffn1_w_down, norm_mix, w_in, conv_a_w, a_log_a, dt_bias_a, norm_a_w, conv_b_w, conv_b_b, a_log_b, dt_bias_b, d_skip_b, norm_b_w, conv_c_w, conv_c_b, w_rgate, b_rgate, w_igate, b_igate, lru_lambda, w_out, norm_ffn2, ffn2_w_gate, ffn2_w_up, ffn2_w_down, norm_final):
    params = {
        'norm_ffn1': norm_ffn1, 'ffn1_w_gate': ffn1_w_gate, 'ffn1_w_up': ffn1_w_up, 'ffn1_w_down': ffn1_w_down,
        'norm_mix': norm_mix, 'w_in': w_in,
        'conv_a_w': conv_a_w, 'a_log_a': a_log_a, 'dt_bias_a': dt_bias_a, 'norm_a_w': norm_a_w,
        'conv_b_w': conv_b_w, 'conv_b_b': conv_b_b, 'a_log_b': a_log_b, 'dt_bias_b': dt_bias_b,
        'd_skip_b': d_skip_b, 'norm_b_w': norm_b_w,
        'conv_c_w': conv_c_w, 'conv_c_b': conv_c_b, 'w_rgate': w_rgate, 'b_rgate': b_rgate,
        'w_igate': w_igate, 'b_igate': b_igate, 'lru_lambda': lru_lambda,
        'w_out': w_out,
        'norm_ffn2': norm_ffn2, 'ffn2_w_gate': ffn2_w_gate, 'ffn2_w_up': ffn2_w_up, 'ffn2_w_down': ffn2_w_down,
    }
    layers = [_layer_params({k: v[i] for k, v in params.items() if k != 'w_in'}, w_in, i) for i in range(DEPTH)]
    bp = x_prompt.shape[0]
    zero_states = (
        jnp.zeros((DEPTH, bp, A_HEADS, A_DK, A_DV), F32),
        jnp.zeros((DEPTH, bp, CONV_W - 1, A_CONV_CH), F32),
        jnp.zeros((DEPTH, bp, B_HEADS, B_STATE, B_HEADDIM), F32),
        jnp.zeros((DEPTH, bp, CONV_W - 1, B_CONV_CH), F32),
        jnp.zeros((DEPTH, bp, C_WIDTH), F32),
        jnp.zeros((DEPTH, bp, CONV_W - 1, C_WIDTH), F32),
    )
    y_prompt, p_states = _trunk(x_prompt, zero_states, layers, norm_final, chunk=128, tb=256, bb=1, tm=512)
    sample_states = (state_delta_s, state_delta_conv, state_ssd_h, state_ssd_conv, state_lru_h, state_lru_conv)
    y_sample, s_states = _trunk(x_sample, sample_states, layers, norm_final, chunk=64, tb=64, bb=4, tm=512)
    return (y_prompt, y_sample) + tuple(p_states) + tuple(s_states)
```

```python
import functools

import jax
import jax.numpy as jnp
from jax import lax
from jax.experimental import pallas as pl
from jax.experimental.pallas import tpu as pltpu

F32 = jnp.float32
BF16 = jnp.bfloat16

D_MODEL = 1024
DEPTH = 2
D_FF = 2816
EPS = 1e-6
CONV_W = 4
A_HEADS = 6
A_DK = 64
A_DV = 64
A_WIDTH = A_HEADS * A_DV
A_CONV_CH = 2 * A_HEADS * A_DK + A_WIDTH
B_HEADS = 6
B_HEADDIM = 64
B_WIDTH = B_HEADS * B_HEADDIM
B_GROUPS = 2
B_STATE = 64
B_CONV_CH = B_WIDTH + 2 * B_GROUPS * B_STATE
C_WIDTH = 256
C_BLOCKS = 8
C_BLOCK = C_WIDTH // C_BLOCKS
LRU_C = 8.0
D_MIX = A_WIDTH + B_WIDTH + C_WIDTH

LANE = 128
SUBLANE = 8
MXU_TILE = 256
assert D_FF % MXU_TILE == 0
HEAD = 64
N_PAIR = A_HEADS // 2
INV_BASE = 16
OUT_STAGE_THUNKS = 10

CONV_CH = A_CONV_CH + B_CONV_CH + C_WIDTH
K_OFF = A_HEADS * A_DK
V_OFF = 2 * A_HEADS * A_DK
XBC_OFF = A_CONV_CH
BM_OFF = XBC_OFF + B_WIDTH
CM_OFF = BM_OFF + B_GROUPS * B_STATE
XC_OFF = A_CONV_CH + B_CONV_CH
GATE_CH = A_WIDTH + B_WIDTH + C_WIDTH + LANE
ZA_OFF, ZB_OFF, GC_OFF, SM_OFF = 0, A_WIDTH, A_WIDTH + B_WIDTH, A_WIDTH + B_WIDTH + C_WIDTH
D_IN_R = CONV_CH + GATE_CH
BETA_L, ALPHA_L, DT_L = 0, A_HEADS, 2 * A_HEADS
PV_ROWS = 8
PV_NORM, PV_CW, PV_CB, PV_MISC = 0, 1, 5, 6
M_BIAS, M_ALOG, M_NA, M_DSKIP, M_NB, M_LAM, M_BRI = 0, 128, 256, 384, 768, 1152, 1408

VMEM_LIMIT = 56 * 1024 * 1024


def _dot(a, b):
    return jnp.dot(a, b, preferred_element_type=F32)


def _dot_nt(a, b):
    return lax.dot_general(a, b, (((1,), (1,)), ((), ())), preferred_element_type=F32)


def _dot_tn(a, b):
    return lax.dot_general(a, b, (((0,), (0,)), ((), ())), preferred_element_type=F32)


def _split(x, n):
    parts = []
    r = x
    for _ in range(n):
        p = r.astype(BF16)
        parts.append(p)
        r = r - p.astype(F32)
    return parts


def _dot_left_exact(m, x, n):
    return sum(_dot(m, p) for p in _split(x, n))


def _dot_right_exact(x, m, n):
    return sum(_dot(p, m) for p in _split(x, n))


def _sigmoid(x):
    return jax.nn.sigmoid(x)


def _silu(x):
    return x * _sigmoid(x)


def _softplus(x):
    return jnp.maximum(x, 0.0) + jnp.log1p(jnp.exp(-jnp.abs(x)))


def _gelu_tanh(x):
    return 0.5 * x * (1.0 + jnp.tanh(0.7978845608028654 * (x + 0.044715 * x * x * x)))


def _rms(x, w):
    ms = jnp.mean(x * x, axis=-1, keepdims=True)
    return x * lax.rsqrt(ms + EPS) * w


def _ffn_body(x_ref, nw_ref, wg_ref, wu_ref, wd_ref, fnw_ref, o_ref, n_split):
    x = x_ref[...]
    h = _rms(x, nw_ref[...]).astype(BF16)
    tiles = D_FF // MXU_TILE
    bounds = [MXU_TILE * ((tiles * j + n_split - 1) // n_split) for j in range(n_split)] + [D_FF]
    acc = None
    for lo_, hi_ in zip(bounds[:-1], bounds[1:]):
        g = _dot(h, wg_ref[:, lo_:hi_])
        u = _dot(h, wu_ref[:, lo_:hi_])
        a = (_silu(g) * u).astype(BF16)
        d = _dot(a, wd_ref[lo_:hi_, :])
        acc = d if acc is None else acc + d
    y = x + 0.5 * acc
    if fnw_ref is not None:
        y = _rms(y, fnw_ref[...])
    o_ref[...] = y


def _ffn_kernel(x_ref, nw_ref, wg_ref, wu_ref, wd_ref, o_ref, *, n_split):
    _ffn_body(x_ref, nw_ref, wg_ref, wu_ref, wd_ref, None, o_ref, n_split)


def _ffn_final_kernel(x_ref, nw_ref, wg_ref, wu_ref, wd_ref, fnw_ref, o_ref, *, n_split):
    _ffn_body(x_ref, nw_ref, wg_ref, wu_ref, wd_ref, fnw_ref, o_ref, n_split)


def _resident(shape, layer=None):
    if layer is None:
        return pl.BlockSpec(shape, lambda *_: (0,) * len(shape), pipeline_mode=pl.Buffered(1))
    return pl.BlockSpec((None,) + tuple(shape), lambda *_: (layer,) + (0,) * len(shape),
                        pipeline_mode=pl.Buffered(1))


def _ffn(x2d, nw, wg, wu, wd, final_w=None, *, tm, layer):
    t = x2d.shape[0]
    assert t % tm == 0
    row = pl.BlockSpec((tm, D_MODEL), lambda i: (i, 0))
    in_specs = [row, _resident((1, D_MODEL)), _resident((D_MODEL, D_FF), layer), _resident((D_MODEL, D_FF), layer),
                _resident((D_FF, D_MODEL), layer)]
    args = [x2d, nw, wg, wu, wd]
    if final_w is None:
        body = functools.partial(_ffn_kernel, n_split=2)
    else:
        body = functools.partial(_ffn_final_kernel, n_split=2)
        in_specs.append(_resident((1, D_MODEL)))
        args.append(final_w)
    return pl.pallas_call(
        body,
        out_shape=jax.ShapeDtypeStruct((t, D_MODEL), F32),
        grid=(t // tm,),
        in_specs=in_specs,
        out_specs=row,
        compiler_params=pltpu.CompilerParams(dimension_semantics=("arbitrary",), vmem_limit_bytes=VMEM_LIMIT),
        name="ffn_final" if final_w is not None else "ffn",
    )(*args)


def _wide_blockdiag(xw, c):
    if c % LANE == 0:
        z = jnp.zeros((c, c), BF16)
        xb = xw.astype(BF16)
        return jnp.concatenate([jnp.concatenate([xb[:, :c], z], axis=1),
                                jnp.concatenate([z, xb[:, c:]], axis=1)], axis=0)
    first = lax.broadcasted_iota(jnp.int32, (1, 2 * c), 1) < c
    return jnp.concatenate([jnp.where(first, xw, 0.0).astype(BF16), jnp.where(first, 0.0, xw).astype(BF16)], axis=0)


def _head_split_rows(x, lo):
    return jnp.concatenate([jnp.where(lo, x, 0.0), jnp.where(lo, 0.0, x)], axis=0)


def _tri_inv_wide(lws, c, between):
    w = 2 * c
    nb = c // INV_BASE
    ri = lax.broadcasted_iota(jnp.int32, (c, w), 0)
    cj = lax.broadcasted_iota(jnp.int32, (c, w), 1) % c
    lane_w = lax.broadcasted_iota(jnp.int32, (1, w), 1)
    lane_blk = (lane_w % c) // INV_BASE
    blk_rows = [jnp.where(lane_w // INV_BASE == kb, 1.0, 0.0).astype(BF16) for kb in range(2 * nb)]

    def to_slab(x):
        return sum(jnp.where(lane_blk == b, x[b * INV_BASE:(b + 1) * INV_BASE, :], 0.0) for b in range(nb))

    def slab_blockdiag(sl):
        sb = sl.astype(BF16)
        return jnp.concatenate([sb * m for m in blk_rows], axis=0)

    def from_slab(sl):
        return jnp.concatenate([jnp.where(lane_blk == b, sl, 0.0) for b in range(nb)], axis=0)

    eye_slab = jnp.where(lax.broadcasted_iota(jnp.int32, (INV_BASE, w), 0) == lane_w % INV_BASE, 1.0, 0.0)
    ns = [-to_slab(lw) for lw in lws]
    ts = [eye_slab + n for n in ns]
    ps = [_dot(n.astype(BF16), slab_blockdiag(n)) for n in ns]
    between()
    n_pow = 2
    while 2 * n_pow < INV_BASE:
        outs = [_dot(jnp.concatenate([p.astype(BF16), t.astype(BF16)], axis=0), slab_blockdiag(p))
                for p, t in zip(ps, ts)]
        between()
        ts = [t + o[INV_BASE:] for t, o in zip(ts, outs)]
        ps = [o[:INV_BASE] for o in outs]
        n_pow *= 2
    ts = [t + _dot(t.astype(BF16), slab_blockdiag(p)) for t, p in zip(ts, ps)]
    between()
    ts = [from_slab(t) for t in ts]

    def mm(a, bw):
        return _dot(a.astype(BF16), _wide_blockdiag(bw, c))

    s = INV_BASE
    while s < c:
        pairs = range(c // (2 * s))
        odd = lambda x: jnp.concatenate([x[(2 * i + 1) * s:(2 * i + 2) * s] for i in pairs], axis=0)
        sub = (ri // (2 * s) == cj // (2 * s)) & ((ri // s) % 2 == 1) & ((cj // s) % 2 == 0)
        ets = [mm(odd(jnp.where(sub, lw, 0.0)), t) for lw, t in zip(lws, ts)]
        between()
        zero = jnp.zeros((s, w), F32)
        spread = lambda x: jnp.concatenate([y for i in pairs for y in (zero, x[i * s:(i + 1) * s])], axis=0)
        news = [odd(t) - mm(odd(t), spread(et)) for t, et in zip(ts, ets)]
        between()
        ts = [jnp.concatenate([y for i in pairs for y in (t[2 * i * s:(2 * i + 1) * s], nw[i * s:(i + 1) * s])],
                              axis=0) for t, nw in zip(ts, news)]
        s *= 2
    return ts


def _mixer_kernel(x_ref, xn_ref, pv_ref, win_ref, wri_ref, wout_ref, n1_ref, wg_ref, wu_ref, wd_ref,
                  s0_ref, h0_ref, l0_ref, c0_ref,
                  xo_ref, s_ref, hs_ref, l_ref, co_ref,
                  cbuf, act, gates, gact, scal, obuf, mixo, hbuf, hbuf1, gbuf, abuf, facc, x2c, x2n,
                  *, chunk, tb, bb, n_t):
    c = chunk
    nck = tb // c
    t_idx = pl.program_id(1)
    n_piece = pl.cdiv(D_IN_R, MXU_TILE)
    conv_pieces = CONV_CH // MXU_TILE
    ff_tiles = D_FF // MXU_TILE

    def front_pieces(src_ref, x2dst):
        def norm1():
            for sq in range(bb):
                hbuf1[sq * tb:(sq + 1) * tb, :] = _rms(src_ref[sq], n1_ref[...]).astype(BF16)

        def gate(j):
            gbuf[...] = _dot(hbuf1[...], wg_ref[:, j * MXU_TILE:(j + 1) * MXU_TILE])

        def up(j):
            u = _dot(hbuf1[...], wu_ref[:, j * MXU_TILE:(j + 1) * MXU_TILE])
            abuf[j % 2] = (_silu(gbuf[...]) * u).astype(BF16)

        def down(j):
            d = _dot(abuf[j % 2], wd_ref[j * MXU_TILE:(j + 1) * MXU_TILE, :])
            facc[...] = d if j == 0 else facc[...] + d

        def norm2():
            for sq in range(bb):
                x2 = src_ref[sq] + 0.5 * facc[sq * tb:(sq + 1) * tb, :]
                x2dst[sq] = x2
                hbuf[sq * tb:(sq + 1) * tb, :] = _rms(x2, pv_ref[PV_NORM:PV_NORM + 1, 0:D_MODEL]).astype(BF16)

        def piece(j):
            lo_, hi_ = j * MXU_TILE, min((j + 1) * MXU_TILE, D_IN_R)
            res = _dot(hbuf[...], win_ref[:, lo_:hi_])
            for sq in range(bb):
                part = res[sq * tb:(sq + 1) * tb]
                if j < conv_pieces:
                    cbuf[sq, SUBLANE:SUBLANE + tb, lo_:hi_] = part
                else:
                    gates[sq, :, lo_ - CONV_CH:hi_ - CONV_CH] = part

        thunks = [norm1]
        for j in range(ff_tiles):
            thunks += [functools.partial(gate, j), functools.partial(up, j)]
            if j > 0:
                thunks.append(functools.partial(down, j - 1))
        thunks += [functools.partial(down, ff_tiles - 1), norm2]
        return (thunks + [functools.partial(piece, j) for j in range(conv_pieces)],
                [functools.partial(piece, j) for j in range(conv_pieces, n_piece)])

    @pl.when(t_idx == 0)
    def _():
        s_ref[...] = s0_ref[...]
        hs_ref[...] = h0_ref[...]
        l_ref[...] = l0_ref[...]
        cbuf[:, 0:SUBLANE, :] = c0_ref[...]
        for thunk in front_pieces(x_ref, x2c)[0]:
            thunk()

    if n_t > 1:
        @pl.when(t_idx > 0)
        def _():
            x2c[...] = x2n[...]

    ahead, gate_pieces = front_pieces(xn_ref, x2n)
    fillers = gate_pieces + (ahead if n_t > 1 else [])
    n_units = bb * nck * N_PAIR
    conv_hooks, gate_hooks, out_hooks = CONV_CH // LANE, SM_OFF // LANE, 6 * bb
    mid_hooks = (bb * nck + 4 * n_units + 4 + 2 * max(0, (c // INV_BASE).bit_length() - 1)
                 + nck * (2 + bb * N_PAIR))
    assert len(gate_pieces) <= conv_hooks
    left = max(0, len(fillers) - conv_hooks)
    out_n = min(OUT_STAGE_THUNKS, left)
    gate_n = min(gate_hooks // 2, left - out_n)
    rate = [1.0]
    credit = [0.0]

    def fill():
        credit[0] += rate[0]
        while credit[0] >= 1.0 and fillers:
            fillers.pop(0)()
            credit[0] -= 1.0
    misc = pv_ref[PV_MISC:PV_MISC + 1, :]
    lane1 = lax.broadcasted_iota(jnp.int32, (1, LANE), 1)
    lo = lane1 < HEAD
    lo2 = (lax.broadcasted_iota(jnp.int32, (1, 2 * LANE), 1) % LANE) < HEAD
    dec_lanes = (lane1 >= ALPHA_L) & (lane1 < DT_L + B_HEADS)
    nega = jnp.where(dec_lanes, -jnp.exp(misc[:, M_ALOG:M_ALOG + LANE]), 0.0)
    r128 = lax.broadcasted_iota(jnp.int32, (LANE, LANE), 0)
    c128 = lax.broadcasted_iota(jnp.int32, (LANE, LANE), 1)
    same_head = (r128 // HEAD) == (c128 // HEAD)
    ones_head = jnp.where(same_head, 1.0, 0.0).astype(BF16)
    row_lo = r128 < HEAD

    def head_sum(v):
        return _dot_right_exact(v, ones_head, 1)

    first = SUBLANE - (CONV_W - 1)
    tiles_per_piece = MXU_TILE // LANE
    for j in range(CONV_CH // LANE):
        cs = slice(j * LANE, (j + 1) * LANE)
        for sq in range(bb):
            acc = cbuf[sq, first:first + tb, cs] * pv_ref[PV_CW:PV_CW + 1, cs]
            for k in range(1, CONV_W):
                acc = acc + cbuf[sq, first + k:first + k + tb, cs] * pv_ref[PV_CW + k:PV_CW + k + 1, cs]
            if j * LANE >= XBC_OFF:
                acc = acc + pv_ref[PV_CB:PV_CB + 1, cs]
            if j * LANE < XC_OFF:
                acc = _silu(acc)
            if j * LANE < V_OFF:
                acc = acc * lax.rsqrt(head_sum(acc * acc) + EPS)
                if j * LANE < K_OFF:
                    acc = acc * (A_DK ** -0.5)
            act[sq, :, cs] = acc
        if j % tiles_per_piece == tiles_per_piece - 1:
            ps_ = slice((j + 1 - tiles_per_piece) * LANE, (j + 1) * LANE)
            for sq in range(bb):
                tail = cbuf[sq, tb:tb + SUBLANE, ps_]
                co_ref[sq, :, ps_] = tail
                cbuf[sq, 0:SUBLANE, ps_] = tail
        fill()
    assert len(fillers) <= (len(ahead) if n_t > 1 else 0)
    rate[0] = gate_n / gate_hooks
    for j in range(SM_OFF // LANE):
        cs = slice(j * LANE, (j + 1) * LANE)
        for sq in range(bb):
            gv = gates[sq, :, cs]
            gact[sq, :, cs] = _silu(gv) if j * LANE < GC_OFF else _gelu_tanh(gv)
        fill()
    assert n_t == 1 or len(fillers) >= conv_pieces
    rate[0] = max(0, len(fillers) - out_n) / mid_hooks
    credit[0] = 0.0
    for sq in range(bb):
        sm = gates[sq, :, SM_OFF:SM_OFF + LANE]
        sp = _softplus(sm + misc[:, M_BIAS:M_BIAS + LANE])
        scal[sq, 0] = _sigmoid(sm)
        scal[sq, 1] = sp
        scal[sq, 2] = sp * nega

    ri = lax.broadcasted_iota(jnp.int32, (c, 2 * c), 0)
    cjw = lax.broadcasted_iota(jnp.int32, (c, 2 * c), 1)
    first_w = cjw < c
    cj = cjw % c
    causal_w = ri >= cj
    strict_w = ri > cj
    tril = jnp.where(lax.broadcasted_iota(jnp.int32, (c, c), 0) >= lax.broadcasted_iota(jnp.int32, (c, c), 1),
                     1.0, 0.0).astype(BF16)

    items = [(sq, ck) for sq in range(bb) for ck in range(nck)]
    rows = {it: slice(it[1] * c, (it[1] + 1) * c) for it in items}

    sc = {}
    for it in items:
        sq, rs = it[0], rows[it]
        g = _dot_left_exact(tril, scal[sq, 2, rs, :], 2)
        gt2 = jnp.concatenate([g, g], axis=0).T
        glast = g[c - 1:c, :]
        sc[it] = dict(g=g, gt2=gt2, eg=jnp.exp(g), egl=jnp.exp(glast - g), gtot=jnp.exp(glast))
        fill()

    def pairvec(arr, la_, lb_):
        return jnp.where(lo, arr[:, la_:la_ + 1], arr[:, lb_:lb_ + 1])

    def decay_w(d, la_, lb_):
        gcol = jnp.where(first_w, d['g'][:, la_:la_ + 1], d['g'][:, lb_:lb_ + 1])
        grow = jnp.where(first_w[0:1], d['gt2'][la_:la_ + 1, :], d['gt2'][lb_:lb_ + 1, :])
        return jnp.exp(jnp.where(causal_w, gcol - grow, -jnp.inf))

    def state_scale(d, la_, lb_):
        return jnp.where(row_lo, d['gtot'][:, la_:la_ + 1], d['gtot'][:, lb_:lb_ + 1])

    units = [(it, p) for it in items for p in range(N_PAIR)]

    ga = {}
    for un in units:
        (sq, _), p = un
        rs = rows[un[0]]
        qn = act[sq, rs, p * LANE:(p + 1) * LANE]
        kn = act[sq, rs, K_OFF + p * LANE:K_OFF + (p + 1) * LANE]
        ksplit = _head_split_rows(kn, lo).astype(BF16)
        kq = _dot_nt(jnp.concatenate([kn.astype(BF16), qn.astype(BF16)], axis=0), ksplit)
        ga[un] = dict(qn=qn, kn=kn, kq=kq)
        fill()
    lws = []
    for un in units:
        (sq, _), p = un
        rs, d, a = rows[un[0]], sc[un[0]], ga[un]
        ha, hb_ = 2 * p, 2 * p + 1
        gam = decay_w(d, ALPHA_L + ha, ALPHA_L + hb_)
        beta_c = scal[sq, 0, rs, :]
        bcol = jnp.where(first_w, beta_c[:, BETA_L + ha:BETA_L + ha + 1], beta_c[:, BETA_L + hb_:BETA_L + hb_ + 1])
        lws.append(jnp.where(strict_w, bcol * a['kq'][:c] * gam, 0.0))
        a['qk'] = (a['kq'][c:] * gam).astype(BF16)
        bvec = pairvec(beta_c, BETA_L + ha, BETA_L + hb_)
        egv = pairvec(d['eg'], ALPHA_L + ha, ALPHA_L + hb_)
        eglv = pairvec(d['egl'], ALPHA_L + ha, ALPHA_L + hb_)
        vp = act[sq, rs, V_OFF + p * LANE:V_OFF + (p + 1) * LANE]
        rhs = jnp.concatenate([bvec * vp, bvec * egv * a['kn']], axis=1)
        a['rhs'] = _head_split_rows(rhs, lo2).astype(BF16)
        a['qd'] = a['qn'] * egv
        a['kd'] = (a['kn'] * eglv).astype(BF16)
        del a['kq']
        fill()

    gb = {}
    for it in items:
        sq, rs = it[0], rows[it]
        bm = act[sq, rs, BM_OFF:BM_OFF + LANE]
        cm = act[sq, rs, CM_OFF:CM_OFF + LANE]
        bm_sw = pltpu.roll(bm, HEAD, 1)
        cm_sw = pltpu.roll(cm, HEAD, 1)
        for p in range(N_PAIR):
            ha, hb_ = 2 * p, 2 * p + 1
            g0, g1 = ha // (B_HEADS // B_GROUPS), hb_ // (B_HEADS // B_GROUPS)
            if g0 == g1 == 0:
                bsel, csel = jnp.where(lo, bm, bm_sw), jnp.where(lo, cm, cm_sw)
            elif g0 == g1 == 1:
                bsel, csel = jnp.where(lo, bm_sw, bm), jnp.where(lo, cm_sw, cm)
            else:
                bsel, csel = bm, cm
            gb[(it, p)] = dict(bsel=bsel, csel=csel,
                               scores=_dot_nt(csel.astype(BF16), _head_split_rows(bsel, lo).astype(BF16)))
            fill()
    for un in units:
        (sq, _), p = un
        rs, d, b = rows[un[0]], sc[un[0]], gb[un]
        ha, hb_ = 2 * p, 2 * p + 1
        xs = act[sq, rs, XBC_OFF + p * LANE:XBC_OFF + (p + 1) * LANE]
        xdt = xs * pairvec(scal[sq, 1, rs, :], DT_L + ha, DT_L + hb_)
        b['xb'] = xdt.astype(BF16)
        mw = (b['scores'] * decay_w(d, DT_L + ha, DT_L + hb_)).astype(BF16)
        b['y'] = _dot(mw, _head_split_rows(xdt, lo).astype(BF16))
        b['cd'] = (b['csel'] * pairvec(d['eg'], DT_L + ha, DT_L + hb_)).astype(BF16)
        b['bd'] = (b['bsel'] * pairvec(d['egl'], DT_L + ha, DT_L + hb_)).astype(BF16)
        b['xs'] = xs
        del b['scores'], b['csel'], b['bsel']
        fill()

    tws = _tri_inv_wide(lws, c, fill)
    for un, tw in zip(units, tws):
        ga[un]['uw'] = _dot(tw.astype(BF16), ga[un]['rhs'])

    for ck in range(nck):
        cur = [((sq, ck), p) for sq in range(bb) for p in range(N_PAIR)]
        s_old = {un: s_ref[un[0][0], un[1]] for un in cur}
        h_old = {un: hs_ref[un[0][0], un[1]] for un in cur}
        wq = {un: _dot(jnp.concatenate([ga[un]['uw'][:, LANE:].astype(BF16), ga[un]['qd'].astype(BF16)], axis=0),
                       s_old[un].astype(BF16)) for un in cur}
        fill()
        yi = {un: _dot(gb[un]['cd'], h_old[un].astype(BF16)) for un in cur}
        fill()
        for un in cur:
            (sq, _), p = un
            rs, d, a, b = rows[un[0]], sc[un[0]], ga[un], gb[un]
            ha, hb_ = 2 * p, 2 * p + 1
            delta = a['uw'][:, :LANE] - wq[un][:c]
            o = wq[un][c:] + _dot(a['qk'], _head_split_rows(delta, lo).astype(BF16))
            upd = _dot_tn(a['kd'], delta.astype(BF16))
            s_ref[sq, p] = state_scale(d, ALPHA_L + ha, ALPHA_L + hb_) * s_old[un] + jnp.where(same_head, upd, 0.0)
            obuf[sq, rs, p * LANE:(p + 1) * LANE] = o
            updh = _dot_tn(b['bd'], b['xb'])
            hs_ref[sq, p] = state_scale(d, DT_L + ha, DT_L + hb_) * h_old[un] + jnp.where(same_head, updh, 0.0)
            y = b['y'] + yi[un] + misc[:, M_DSKIP + p * LANE:M_DSKIP + (p + 1) * LANE] * b['xs']
            obuf[sq, rs, A_WIDTH + p * LANE:A_WIDTH + (p + 1) * LANE] = (
                y * gact[sq, rs, ZB_OFF + p * LANE:ZB_OFF + (p + 1) * LANE])
            fill()

    g3r = lax.broadcasted_iota(jnp.int32, (B_WIDTH, B_WIDTH), 0) // (B_WIDTH // B_GROUPS)
    g3c = lax.broadcasted_iota(jnp.int32, (B_WIDTH, B_WIDTH), 1) // (B_WIDTH // B_GROUPS)
    ones_group = jnp.where(g3r == g3c, 1.0, 0.0).astype(BF16)
    rowt = lax.broadcasted_iota(jnp.int32, (tb, LANE), 0)
    rate[0] = len(fillers) / out_hooks
    credit[0] = 0.0
    for sq in range(bb):
        ms_ = slice(sq * tb, (sq + 1) * tb)
        for p in range(N_PAIR):
            o = obuf[sq, :, p * LANE:(p + 1) * LANE]
            ms = head_sum(o * o) * (1.0 / A_DV)
            on = o * lax.rsqrt(ms + EPS) * misc[:, M_NA:M_NA + LANE]
            mixo[ms_, p * LANE:(p + 1) * LANE] = (
                on * gact[sq, :, ZA_OFF + p * LANE:ZA_OFF + (p + 1) * LANE]).astype(BF16)
            fill()
        yb = obuf[sq, :, A_WIDTH:A_WIDTH + B_WIDTH]
        ms = _dot_right_exact(yb * yb, ones_group, 1) * (1.0 / (B_WIDTH // B_GROUPS))
        ob = yb * lax.rsqrt(ms + EPS) * misc[:, M_NB:M_NB + B_WIDTH]
        mixo[ms_, A_WIDTH:A_WIDTH + B_WIDTH] = ob.astype(BF16)
        fill()

        xc = act[sq, :, XC_OFF:XC_OFF + C_WIDTH]
        rig = _dot(xc.astype(BF16), wri_ref[...]) + misc[:, M_BRI:M_BRI + 2 * C_WIDTH]
        for half in range(C_WIDTH // LANE):
            hs_ = slice(half * LANE, (half + 1) * LANE)
            xch = xc[:, hs_]
            lam = misc[:, M_LAM + half * LANE:M_LAM + (half + 1) * LANE]
            log_a = -LRU_C * _sigmoid(rig[:, hs_]) * _softplus(-lam)
            a = jnp.exp(log_a)
            b = jnp.sqrt(1.0 - jnp.exp(2.0 * log_a)) * (
                _sigmoid(rig[:, C_WIDTH + half * LANE:C_WIDTH + (half + 1) * LANE]) * xch)
            d = 1
            while d < tb:
                keep = rowt >= d
                a_sh = jnp.where(keep, pltpu.roll(a, d, 0), 1.0)
                b_sh = jnp.where(keep, pltpu.roll(b, d, 0), 0.0)
                b = a * b_sh + b
                a = a * a_sh
                d *= 2
            hseq = a * l_ref[sq, :, hs_] + b
            l_ref[sq, :, hs_] = hseq[tb - 1:tb, :]
            mixo[ms_, A_WIDTH + B_WIDTH + half * LANE:A_WIDTH + B_WIDTH + (half + 1) * LANE] = (
                hseq * gact[sq, :, GC_OFF + half * LANE:GC_OFF + (half + 1) * LANE]).astype(BF16)
            fill()

    while fillers:
        fillers.pop(0)()
    mix = _dot(mixo[...], wout_ref[...])
    for sq in range(bb):
        xo_ref[sq] = x2c[sq] + mix[sq * tb:(sq + 1) * tb]


def _mixer(x, pv, win, wri, wout, n1, wg, wu, wd, s0, h0, l0, c0, *, chunk, tb, bb, layer):
    bsz, l, _ = x.shape
    assert l % tb == 0 and tb % chunk == 0 and bsz % bb == 0
    n_t = l // tb
    seq = lambda b, t: (b, 0, 0)
    seq4 = lambda b, t: (b, 0, 0, 0)
    in_specs = [
        pl.BlockSpec((bb, tb, D_MODEL), lambda b, t: (b, t, 0)),
        pl.BlockSpec((bb, tb, D_MODEL), lambda b, t: (b, jnp.minimum(t + 1, n_t - 1), 0)),
        _resident((PV_ROWS, CONV_CH)),
        _resident((D_MODEL, D_IN_R)),
        _resident((C_WIDTH, 2 * C_WIDTH)),
        _resident((D_MIX, D_MODEL), layer),
        _resident((1, D_MODEL)),
        _resident((D_MODEL, D_FF), layer),
        _resident((D_MODEL, D_FF), layer),
        _resident((D_FF, D_MODEL), layer),
        pl.BlockSpec((bb, N_PAIR, LANE, LANE), seq4),
        pl.BlockSpec((bb, N_PAIR, LANE, LANE), seq4),
        pl.BlockSpec((bb, 1, C_WIDTH), seq),
        pl.BlockSpec((bb, SUBLANE, CONV_CH), seq),
    ]
    out_specs = [
        pl.BlockSpec((bb, tb, D_MODEL), lambda b, t: (b, t, 0)),
        pl.BlockSpec((bb, N_PAIR, LANE, LANE), seq4),
        pl.BlockSpec((bb, N_PAIR, LANE, LANE), seq4),
        pl.BlockSpec((bb, 1, C_WIDTH), seq),
        pl.BlockSpec((bb, SUBLANE, CONV_CH), seq),
    ]
    out_shape = [
        jax.ShapeDtypeStruct((bsz, l, D_MODEL), F32),
        jax.ShapeDtypeStruct((bsz, N_PAIR, LANE, LANE), F32),
        jax.ShapeDtypeStruct((bsz, N_PAIR, LANE, LANE), F32),
        jax.ShapeDtypeStruct((bsz, 1, C_WIDTH), F32),
        jax.ShapeDtypeStruct((bsz, SUBLANE, CONV_CH), F32),
    ]
    scratch = [
        pltpu.VMEM((bb, tb + SUBLANE, CONV_CH), F32),
        pltpu.VMEM((bb, tb, CONV_CH), F32),
        pltpu.VMEM((bb, tb, GATE_CH), F32),
        pltpu.VMEM((bb, tb, SM_OFF), F32),
        pltpu.VMEM((bb, 3, tb, LANE), F32),
        pltpu.VMEM((bb, tb, A_WIDTH + B_WIDTH), F32),
        pltpu.VMEM((bb * tb, D_MIX), BF16),
        pltpu.VMEM((bb * tb, D_MODEL), BF16),
        pltpu.VMEM((bb * tb, D_MODEL), BF16),
        pltpu.VMEM((bb * tb, MXU_TILE), F32),
        pltpu.VMEM((2, bb * tb, MXU_TILE), BF16),
        pltpu.VMEM((bb * tb, D_MODEL), F32),
        pltpu.VMEM((bb, tb, D_MODEL), F32),
        pltpu.VMEM((bb, tb, D_MODEL), F32),
    ]
    return pl.pallas_call(
        functools.partial(_mixer_kernel, chunk=chunk, tb=tb, bb=bb, n_t=n_t),
        out_shape=out_shape,
        grid=(bsz // bb, l // tb),
        in_specs=in_specs,
        out_specs=out_specs,
        scratch_shapes=scratch,
        compiler_params=pltpu.CompilerParams(dimension_semantics=("arbitrary", "arbitrary"),
                                             vmem_limit_bytes=VMEM_LIMIT),
        name="mixer_c%d" % chunk,
    )(x, x, pv, win, wri, wout, n1, wg, wu, wd, s0, h0, l0, c0)


def _pair_states(s):
    bsz = s.shape[0]
    s = s.reshape(bsz, N_PAIR, 2, HEAD, HEAD)
    z = jnp.zeros((bsz, N_PAIR, HEAD, HEAD), s.dtype)
    top = jnp.concatenate([s[:, :, 0], z], axis=-1)
    bot = jnp.concatenate([z, s[:, :, 1]], axis=-1)
    return jnp.concatenate([top, bot], axis=-2)


def _unpair_states(sp):
    a = sp[:, :, :HEAD, :HEAD]
    b = sp[:, :, HEAD:, HEAD:]
    return jnp.stack([a, b], axis=2).reshape(sp.shape[0], A_HEADS, HEAD, HEAD)


W_ZA = A_CONV_CH
W_BA = W_ZA + A_WIDTH
W_AA = W_BA + A_HEADS
W_ZB = W_AA + A_HEADS
W_XBC = W_ZB + B_WIDTH
W_DT = W_XBC + B_CONV_CH
W_GC = W_DT + B_HEADS
W_XC = W_GC + C_WIDTH
D_IN = W_XC + C_WIDTH
assert W_BA % LANE == BETA_L and W_AA % LANE == ALPHA_L and W_DT % LANE == DT_L


def _reorder_kernel(w_ref, o_ref):
    def put(dst, src, size):
        o_ref[:, dst:dst + size] = w_ref[0, :, src:src + size].astype(BF16)

    put(0, 0, A_CONV_CH)
    put(XBC_OFF, W_XBC, B_CONV_CH)
    put(XC_OFF, W_XC, C_WIDTH)
    put(CONV_CH + ZA_OFF, W_ZA, A_WIDTH)
    put(CONV_CH + ZB_OFF, W_ZB, B_WIDTH)
    put(CONV_CH + GC_OFF, W_GC, C_WIDTH)
    lane = lax.broadcasted_iota(jnp.int32, (1, LANE), 1)
    ba = w_ref[0, :, W_BA - BETA_L:W_BA - BETA_L + LANE]
    dt = w_ref[0, :, W_DT - DT_L:W_DT - DT_L + LANE]
    small = jnp.where(lane < DT_L, ba, jnp.where(lane < DT_L + B_HEADS, dt, 0.0))
    o_ref[:, CONV_CH + SM_OFF:CONV_CH + SM_OFF + LANE] = small.astype(BF16)


def _reorder_w_in(w_in, layer, rows=256):
    return pl.pallas_call(
        _reorder_kernel,
        out_shape=jax.ShapeDtypeStruct((D_MODEL, D_IN_R), BF16),
        grid=(D_MODEL // rows,),
        in_specs=[pl.BlockSpec((1, rows, D_IN), lambda i: (layer, i, 0))],
        out_specs=pl.BlockSpec((rows, D_IN_R), lambda i: (i, 0)),
        name="reorder_w_in",
    )(w_in)


def _layer_params(lp, w_in, layer):
    win = _reorder_w_in(w_in, layer)

    def lane_block(*pieces):
        v = jnp.zeros((LANE,), F32)
        for off, val in pieces:
            v = lax.dynamic_update_slice(v, val.astype(F32), (off,))
        return v

    misc = jnp.concatenate([
        lane_block((ALPHA_L, lp['dt_bias_a']), (DT_L, lp['dt_bias_b'])),
        lane_block((ALPHA_L, lp['a_log_a']), (DT_L, lp['a_log_b'])),
        jnp.tile(lp['norm_a_w'], 2),
        jnp.repeat(lp['d_skip_b'], B_HEADDIM),
        lp['norm_b_w'],
        lp['lru_lambda'],
        lp['b_rgate'], lp['b_igate'],
    ])
    misc = jnp.pad(misc, (0, CONV_CH - misc.shape[0]))
    cw = jnp.concatenate([lp['conv_a_w'], lp['conv_b_w'], lp['conv_c_w']], axis=1)
    cb = jnp.concatenate([jnp.zeros((A_CONV_CH,), F32), lp['conv_b_b'], lp['conv_c_b']])
    pv = jnp.concatenate([jnp.pad(lp['norm_mix'], (0, CONV_CH - D_MODEL))[None], cw, cb[None], misc[None],
                          jnp.zeros((1, CONV_CH), F32)], axis=0)

    def block_diag(wb):
        eye = jnp.eye(C_BLOCKS, dtype=wb.dtype)
        return jnp.einsum('ncd,nm->ncmd', wb, eye).reshape(C_WIDTH, C_WIDTH)

    wri = jnp.concatenate([block_diag(lp['w_rgate']), block_diag(lp['w_igate'])], axis=1).astype(BF16)
    return dict(pv=pv, win=win, wri=wri, n1=lp['norm_ffn1'][None], n2=lp['norm_ffn2'][None])


def _trunk(x, states, layers, big, norm_final, *, chunk, tb, bb, tm):
    bsz, l, _ = x.shape
    delta_s, delta_conv, ssd_h, ssd_conv, lru_h, lru_conv = states
    outs = [[] for _ in range(6)]
    for layer in range(DEPTH):
        lw = layers[layer]
        conv0 = jnp.concatenate([delta_conv[layer], ssd_conv[layer], lru_conv[layer]], axis=-1).astype(F32)
        conv0 = jnp.pad(conv0, ((0, 0), (SUBLANE - (CONV_W - 1), 0), (0, 0)))
        x3, s_new, h_new, l_new, c_new = _mixer(
            x, lw['pv'], lw['win'], lw['wri'], big['wout'], lw['n1'], big['g1'], big['u1'], big['d1'],
            _pair_states(delta_s[layer].astype(F32)), _pair_states(ssd_h[layer].astype(F32)),
            lru_h[layer].astype(F32)[:, None, :], conv0, chunk=chunk, tb=tb, bb=bb, layer=layer)
        fin = norm_final[None] if layer == DEPTH - 1 else None
        x = _ffn(x3.reshape(bsz * l, D_MODEL), lw['n2'], big['g2'], big['u2'], big['d2'], fin,
                 tm=tm, layer=layer).reshape(bsz, l, D_MODEL)
        c_new = c_new[:, SUBLANE - (CONV_W - 1):, :]
        outs[0].append(_unpair_states(s_new))
        outs[1].append(c_new[..., :A_CONV_CH])
        outs[2].append(_unpair_states(h_new))
        outs[3].append(c_new[..., XBC_OFF:XBC_OFF + B_CONV_CH])
        outs[4].append(l_new[:, 0, :])
        outs[5].append(c_new[..., XC_OFF:])
    return x, tuple(jnp.stack(o) for o in outs)


def kernel(x_prompt, x_sample, state_delta_s, state_delta_conv, state_ssd_h, state_ssd_conv, state_lru_h, state_lru_conv, norm_ffn1, ffn1_w_gate, ffn1_w_up, ffn1_w_down, norm_mix, w_in, conv_a_w, a_log_a, dt_bias_a, norm_a_w, conv_b_w, conv_b_b, a_log_b, dt_bias_b, d_skip_b, norm_b_w, conv_c_w, conv_c_b, w_rgate, b_rgate, w_igate, b_igate, lru_lambda, w_out, norm_ffn2, ffn2_w_gate, ffn2_w_up, ffn2_w_down, norm_final):
    params = {
        'norm_ffn1': norm_ffn1, 'ffn1_w_gate': ffn1_w_gate, 'ffn1_w_up': ffn1_w_up, 'ffn1_w_down': ffn1_w_down,
        'norm_mix': norm_mix, 'w_in': w_in,
        'conv_a_w': conv_a_w, 'a_log_a': a_log_a, 'dt_bias_a': dt_bias_a, 'norm_a_w': norm_a_w,
        'conv_b_w': conv_b_w, 'conv_b_b': conv_b_b, 'a_log_b': a_log_b, 'dt_bias_b': dt_bias_b,
        'd_skip_b': d_skip_b, 'norm_b_w': norm_b_w,
        'conv_c_w': conv_c_w, 'conv_c_b': conv_c_b, 'w_rgate': w_rgate, 'b_rgate': b_rgate,
        'w_igate': w_igate, 'b_igate': b_igate, 'lru_lambda': lru_lambda,
        'w_out': w_out,
        'norm_ffn2': norm_ffn2, 'ffn2_w_gate': ffn2_w_gate, 'ffn2_w_up': ffn2_w_up, 'ffn2_w_down': ffn2_w_down,
    }
    matmul_weights = dict(wout='w_out', g1='ffn1_w_gate', u1='ffn1_w_up', d1='ffn1_w_down',
                          g2='ffn2_w_gate', u2='ffn2_w_up', d2='ffn2_w_down')
    big = {short: params[name].astype(BF16) for short, name in matmul_weights.items()}
    small = {k: v for k, v in params.items() if k != 'w_in' and k not in matmul_weights.values()}
    layers = [_layer_params({k: v[i] for k, v in small.items()}, w_in, i) for i in range(DEPTH)]
    bp = x_prompt.shape[0]
    zero_states = (
        jnp.zeros((DEPTH, bp, A_HEADS, A_DK, A_DV), F32),
        jnp.zeros((DEPTH, bp, CONV_W - 1, A_CONV_CH), F32),
        jnp.zeros((DEPTH, bp, B_HEADS, B_STATE, B_HEADDIM), F32),
        jnp.zeros((DEPTH, bp, CONV_W - 1, B_CONV_CH), F32),
        jnp.zeros((DEPTH, bp, C_WIDTH), F32),
        jnp.zeros((DEPTH, bp, CONV_W - 1, C_WIDTH), F32),
    )
    y_prompt, p_states = _trunk(x_prompt, zero_states, layers, big, norm_final, chunk=128, tb=256, bb=1, tm=512)
    sample_states = (state_delta_s, state_delta_conv, state_ssd_h, state_ssd_conv, state_lru_h, state_lru_conv)
    y_sample, s_states = _trunk(x_sample, sample_states, layers, big, norm_final, chunk=64, tb=64, bb=4, tm=512)
    return (y_prompt, y_sample) + tuple(p_states) + tuple(s_states)
```

```python
import functools

import jax
import jax.numpy as jnp
from jax import lax
from jax.experimental import pallas as pl
from jax.experimental.pallas import tpu as pltpu

F32 = jnp.float32
BF16 = jnp.bfloat16

D_MODEL = 1024
DEPTH = 2
D_FF = 2816
EPS = 1e-6
CONV_W = 4
A_HEADS = 6
A_DK = 64
A_DV = 64
A_WIDTH = A_HEADS * A_DV
A_CONV_CH = 2 * A_HEADS * A_DK + A_WIDTH
B_HEADS = 6
B_HEADDIM = 64
B_WIDTH = B_HEADS * B_HEADDIM
B_GROUPS = 2
B_STATE = 64
B_CONV_CH = B_WIDTH + 2 * B_GROUPS * B_STATE
C_WIDTH = 256
C_BLOCKS = 8
C_BLOCK = C_WIDTH // C_BLOCKS
LRU_C = 8.0
D_MIX = A_WIDTH + B_WIDTH + C_WIDTH

LANE = 128
SUBLANE = 8
MXU_TILE = 256
assert D_FF % MXU_TILE == 0
HEAD = 64
N_PAIR = A_HEADS // 2
INV_BASE = 16
OUT_STAGE_THUNKS = 10

CONV_CH = A_CONV_CH + B_CONV_CH + C_WIDTH
K_OFF = A_HEADS * A_DK
V_OFF = 2 * A_HEADS * A_DK
XBC_OFF = A_CONV_CH
BM_OFF = XBC_OFF + B_WIDTH
CM_OFF = BM_OFF + B_GROUPS * B_STATE
XC_OFF = A_CONV_CH + B_CONV_CH
GATE_CH = A_WIDTH + B_WIDTH + C_WIDTH + LANE
ZA_OFF, ZB_OFF, GC_OFF, SM_OFF = 0, A_WIDTH, A_WIDTH + B_WIDTH, A_WIDTH + B_WIDTH + C_WIDTH
D_IN_R = CONV_CH + GATE_CH
BETA_L, ALPHA_L, DT_L = 0, A_HEADS, 2 * A_HEADS
PV_ROWS = 8
PV_NORM, PV_CW, PV_CB, PV_MISC = 0, 1, 5, 6
M_BIAS, M_ALOG, M_NA, M_DSKIP, M_NB, M_LAM, M_BRI = 0, 128, 256, 384, 768, 1152, 1408

VMEM_LIMIT = 56 * 1024 * 1024


def _dot(a, b):
    return jnp.dot(a, b, preferred_element_type=F32)


def _dot_nt(a, b):
    return lax.dot_general(a, b, (((1,), (1,)), ((), ())), preferred_element_type=F32)


def _dot_tn(a, b):
    return lax.dot_general(a, b, (((0,), (0,)), ((), ())), preferred_element_type=F32)


def _split(x, n):
    parts = []
    r = x
    for _ in range(n):
        p = r.astype(BF16)
        parts.append(p)
        r = r - p.astype(F32)
    return parts


def _dot_left_exact(m, x, n):
    return sum(_dot(m, p) for p in _split(x, n))


def _dot_right_exact(x, m, n):
    return sum(_dot(p, m) for p in _split(x, n))


def _sigmoid(x):
    return 0.5 + 0.5 * jnp.tanh(0.5 * x)


def _silu(x):
    h = 0.5 * x
    return h + h * jnp.tanh(h)


def _softplus(x):
    return jnp.maximum(x, 0.0) + jnp.log1p(jnp.exp(-jnp.abs(x)))


def _gelu_tanh(x):
    return 0.5 * x * (1.0 + jnp.tanh(0.7978845608028654 * (x + 0.044715 * x * x * x)))


def _rms(x, w):
    ms = jnp.mean(x * x, axis=-1, keepdims=True)
    return x * lax.rsqrt(ms + EPS) * w


def _ffn_body(x_ref, nw_ref, wg_ref, wu_ref, wd_ref, fnw_ref, o_ref, n_split):
    x = x_ref[...]
    h = _rms(x, nw_ref[...]).astype(BF16)
    tiles = D_FF // MXU_TILE
    bounds = [MXU_TILE * ((tiles * j + n_split - 1) // n_split) for j in range(n_split)] + [D_FF]
    acc = None
    for lo_, hi_ in zip(bounds[:-1], bounds[1:]):
        g = _dot(h, wg_ref[:, lo_:hi_])
        u = _dot(h, wu_ref[:, lo_:hi_])
        a = (_silu(g) * u).astype(BF16)
        d = _dot(a, wd_ref[lo_:hi_, :])
        acc = d if acc is None else acc + d
    y = x + 0.5 * acc
    if fnw_ref is not None:
        y = _rms(y, fnw_ref[...])
    o_ref[...] = y


def _ffn_kernel(x_ref, nw_ref, wg_ref, wu_ref, wd_ref, o_ref, *, n_split):
    _ffn_body(x_ref, nw_ref, wg_ref, wu_ref, wd_ref, None, o_ref, n_split)


def _ffn_final_kernel(x_ref, nw_ref, wg_ref, wu_ref, wd_ref, fnw_ref, o_ref, *, n_split):
    _ffn_body(x_ref, nw_ref, wg_ref, wu_ref, wd_ref, fnw_ref, o_ref, n_split)


def _resident(shape, layer=None):
    if layer is None:
        return pl.BlockSpec(shape, lambda *_: (0,) * len(shape), pipeline_mode=pl.Buffered(1))
    return pl.BlockSpec((None,) + tuple(shape), lambda *_: (layer,) + (0,) * len(shape),
                        pipeline_mode=pl.Buffered(1))


def _ffn(x2d, nw, wg, wu, wd, final_w=None, *, tm, layer):
    t = x2d.shape[0]
    assert t % tm == 0
    row = pl.BlockSpec((tm, D_MODEL), lambda i: (i, 0))
    in_specs = [row, _resident((1, D_MODEL)), _resident((D_MODEL, D_FF), layer), _resident((D_MODEL, D_FF), layer),
                _resident((D_FF, D_MODEL), layer)]
    args = [x2d, nw, wg, wu, wd]
    if final_w is None:
        body = functools.partial(_ffn_kernel, n_split=2)
    else:
        body = functools.partial(_ffn_final_kernel, n_split=2)
        in_specs.append(_resident((1, D_MODEL)))
        args.append(final_w)
    return pl.pallas_call(
        body,
        out_shape=jax.ShapeDtypeStruct((t, D_MODEL), F32),
        grid=(t // tm,),
        in_specs=in_specs,
        out_specs=row,
        compiler_params=pltpu.CompilerParams(dimension_semantics=("arbitrary",), vmem_limit_bytes=VMEM_LIMIT),
        name="ffn_final" if final_w is not None else "ffn",
    )(*args)


def _wide_blockdiag(xw, c):
    if c % LANE == 0:
        z = jnp.zeros((c, c), BF16)
        xb = xw.astype(BF16)
        return jnp.concatenate([jnp.concatenate([xb[:, :c], z], axis=1),
                                jnp.concatenate([z, xb[:, c:]], axis=1)], axis=0)
    first = lax.broadcasted_iota(jnp.int32, (1, 2 * c), 1) < c
    return jnp.concatenate([jnp.where(first, xw, 0.0).astype(BF16), jnp.where(first, 0.0, xw).astype(BF16)], axis=0)


def _head_split_rows(x, lo):
    return jnp.concatenate([jnp.where(lo, x, 0.0), jnp.where(lo, 0.0, x)], axis=0)


def _tri_inv_wide(lws, c, between):
    w = 2 * c
    nb = c // INV_BASE
    ri = lax.broadcasted_iota(jnp.int32, (c, w), 0)
    cj = lax.broadcasted_iota(jnp.int32, (c, w), 1) % c
    lane_w = lax.broadcasted_iota(jnp.int32, (1, w), 1)
    lane_blk = (lane_w % c) // INV_BASE
    blk_rows = [jnp.where(lane_w // INV_BASE == kb, 1.0, 0.0).astype(BF16) for kb in range(2 * nb)]

    def to_slab(x):
        return sum(jnp.where(lane_blk == b, x[b * INV_BASE:(b + 1) * INV_BASE, :], 0.0) for b in range(nb))

    def slab_blockdiag(sl):
        sb = sl.astype(BF16)
        return jnp.concatenate([sb * m for m in blk_rows], axis=0)

    def from_slab(sl):
        return jnp.concatenate([jnp.where(lane_blk == b, sl, 0.0) for b in range(nb)], axis=0)

    eye_slab = jnp.where(lax.broadcasted_iota(jnp.int32, (INV_BASE, w), 0) == lane_w % INV_BASE, 1.0, 0.0)
    ns = [-to_slab(lw) for lw in lws]
    ts = [eye_slab + n for n in ns]
    ps = [_dot(n.astype(BF16), slab_blockdiag(n)) for n in ns]
    between()
    n_pow = 2
    while 2 * n_pow < INV_BASE:
        outs = [_dot(jnp.concatenate([p.astype(BF16), t.astype(BF16)], axis=0), slab_blockdiag(p))
                for p, t in zip(ps, ts)]
        between()
        ts = [t + o[INV_BASE:] for t, o in zip(ts, outs)]
        ps = [o[:INV_BASE] for o in outs]
        n_pow *= 2
    ts = [t + _dot(t.astype(BF16), slab_blockdiag(p)) for t, p in zip(ts, ps)]
    between()
    ts = [from_slab(t) for t in ts]

    def mm(a, bw):
        return _dot(a.astype(BF16), _wide_blockdiag(bw, c))

    s = INV_BASE
    while s < c:
        pairs = range(c // (2 * s))
        odd = lambda x: jnp.concatenate([x[(2 * i + 1) * s:(2 * i + 2) * s] for i in pairs], axis=0)
        sub = (ri // (2 * s) == cj // (2 * s)) & ((ri // s) % 2 == 1) & ((cj // s) % 2 == 0)
        ets = [mm(odd(jnp.where(sub, lw, 0.0)), t) for lw, t in zip(lws, ts)]
        between()
        zero = jnp.zeros((s, w), F32)
        spread = lambda x: jnp.concatenate([y for i in pairs for y in (zero, x[i * s:(i + 1) * s])], axis=0)
        news = [odd(t) - mm(odd(t), spread(et)) for t, et in zip(ts, ets)]
        between()
        ts = [jnp.concatenate([y for i in pairs for y in (t[2 * i * s:(2 * i + 1) * s], nw[i * s:(i + 1) * s])],
                              axis=0) for t, nw in zip(ts, news)]
        s *= 2
    return ts


def _mixer_kernel(x_ref, xn_ref, pv_ref, win_ref, wri_ref, wout_ref, n1_ref, wg_ref, wu_ref, wd_ref,
                  s0_ref, h0_ref, l0_ref, c0_ref,
                  xo_ref, s_ref, hs_ref, l_ref, co_ref,
                  cbuf, act, gates, gact, scal, obuf, mixo, hbuf, hbuf1, gbuf, abuf, facc, x2c, x2n,
                  *, chunk, tb, bb, n_t):
    c = chunk
    nck = tb // c
    t_idx = pl.program_id(1)
    n_piece = pl.cdiv(D_IN_R, MXU_TILE)
    conv_pieces = CONV_CH // MXU_TILE
    ff_tiles = D_FF // MXU_TILE

    def front_pieces(src_ref, x2dst):
        def norm1():
            for sq in range(bb):
                hbuf1[sq * tb:(sq + 1) * tb, :] = _rms(src_ref[sq], n1_ref[...]).astype(BF16)

        def gate(j):
            gbuf[...] = _dot(hbuf1[...], wg_ref[:, j * MXU_TILE:(j + 1) * MXU_TILE])

        def up(j):
            u = _dot(hbuf1[...], wu_ref[:, j * MXU_TILE:(j + 1) * MXU_TILE])
            abuf[j % 2] = (_silu(gbuf[...]) * u).astype(BF16)

        def down(j):
            d = _dot(abuf[j % 2], wd_ref[j * MXU_TILE:(j + 1) * MXU_TILE, :])
            facc[...] = d if j == 0 else facc[...] + d

        def norm2():
            for sq in range(bb):
                x2 = src_ref[sq] + 0.5 * facc[sq * tb:(sq + 1) * tb, :]
                x2dst[sq] = x2
                hbuf[sq * tb:(sq + 1) * tb, :] = _rms(x2, pv_ref[PV_NORM:PV_NORM + 1, 0:D_MODEL]).astype(BF16)

        def piece(j):
            lo_, hi_ = j * MXU_TILE, min((j + 1) * MXU_TILE, D_IN_R)
            res = _dot(hbuf[...], win_ref[:, lo_:hi_])
            for sq in range(bb):
                part = res[sq * tb:(sq + 1) * tb]
                if j < conv_pieces:
                    cbuf[sq, SUBLANE:SUBLANE + tb, lo_:hi_] = part
                else:
                    gates[sq, :, lo_ - CONV_CH:hi_ - CONV_CH] = part

        thunks = [norm1]
        for j in range(ff_tiles):
            thunks += [functools.partial(gate, j), functools.partial(up, j)]
            if j > 0:
                thunks.append(functools.partial(down, j - 1))
        thunks += [functools.partial(down, ff_tiles - 1), norm2]
        return (thunks + [functools.partial(piece, j) for j in range(conv_pieces)],
                [functools.partial(piece, j) for j in range(conv_pieces, n_piece)])

    @pl.when(t_idx == 0)
    def _():
        s_ref[...] = s0_ref[...]
        hs_ref[...] = h0_ref[...]
        l_ref[...] = l0_ref[...]
        cbuf[:, 0:SUBLANE, :] = c0_ref[...]
        for thunk in front_pieces(x_ref, x2c)[0]:
            thunk()

    if n_t > 1:
        @pl.when(t_idx > 0)
        def _():
            x2c[...] = x2n[...]

    ahead, gate_pieces = front_pieces(xn_ref, x2n)
    fillers = gate_pieces + (ahead if n_t > 1 else [])
    n_units = bb * nck * N_PAIR
    conv_hooks, gate_hooks, out_hooks = CONV_CH // LANE, SM_OFF // LANE, 6 * bb
    mid_hooks = (bb * nck + 4 * n_units + 4 + 2 * max(0, (c // INV_BASE).bit_length() - 1)
                 + nck * (2 + bb * N_PAIR))
    assert len(gate_pieces) <= conv_hooks
    left = max(0, len(fillers) - conv_hooks)
    out_n = min(OUT_STAGE_THUNKS, left)
    gate_n = min(gate_hooks // 2, left - out_n)
    rate = [1.0]
    credit = [0.0]

    def fill():
        credit[0] += rate[0]
        while credit[0] >= 1.0 and fillers:
            fillers.pop(0)()
            credit[0] -= 1.0
    misc = pv_ref[PV_MISC:PV_MISC + 1, :]
    lane1 = lax.broadcasted_iota(jnp.int32, (1, LANE), 1)
    lo = lane1 < HEAD
    lo2 = (lax.broadcasted_iota(jnp.int32, (1, 2 * LANE), 1) % LANE) < HEAD
    dec_lanes = (lane1 >= ALPHA_L) & (lane1 < DT_L + B_HEADS)
    nega = jnp.where(dec_lanes, -jnp.exp(misc[:, M_ALOG:M_ALOG + LANE]), 0.0)
    r128 = lax.broadcasted_iota(jnp.int32, (LANE, LANE), 0)
    c128 = lax.broadcasted_iota(jnp.int32, (LANE, LANE), 1)
    same_head = (r128 // HEAD) == (c128 // HEAD)
    ones_head = jnp.where(same_head, 1.0, 0.0).astype(BF16)
    row_lo = r128 < HEAD

    def head_sum(v):
        return _dot_right_exact(v, ones_head, 1)

    row8 = lax.broadcasted_iota(jnp.int32, (SUBLANE, LANE), 0)
    tiles_per_piece = MXU_TILE // LANE
    for j in range(CONV_CH // LANE):
        cs = slice(j * LANE, (j + 1) * LANE)
        for sq in range(bb):
            xr = cbuf[sq, SUBLANE:SUBLANE + tb, cs]
            head = cbuf[sq, 0:SUBLANE, cs]
            acc = xr * pv_ref[PV_CW + CONV_W - 1:PV_CW + CONV_W, cs]
            for d in range(1, CONV_W):
                rolled = pltpu.roll(xr, d, 0)
                top = jnp.where(row8 < d, pltpu.roll(head, d, 0), rolled[0:SUBLANE])
                shifted = jnp.concatenate([top, rolled[SUBLANE:]], axis=0)
                acc = acc + shifted * pv_ref[PV_CW + CONV_W - 1 - d:PV_CW + CONV_W - d, cs]
            if j * LANE >= XBC_OFF:
                acc = acc + pv_ref[PV_CB:PV_CB + 1, cs]
            if j * LANE < XC_OFF:
                acc = _silu(acc)
            if j * LANE < V_OFF:
                acc = acc * lax.rsqrt(head_sum(acc * acc) + EPS)
                if j * LANE < K_OFF:
                    acc = acc * (A_DK ** -0.5)
            act[sq, :, cs] = acc
        if j % tiles_per_piece == tiles_per_piece - 1:
            ps_ = slice((j + 1 - tiles_per_piece) * LANE, (j + 1) * LANE)
            for sq in range(bb):
                tail = cbuf[sq, tb:tb + SUBLANE, ps_]
                co_ref[sq, :, ps_] = tail
                cbuf[sq, 0:SUBLANE, ps_] = tail
        fill()
    assert len(fillers) <= (len(ahead) if n_t > 1 else 0)
    rate[0] = gate_n / gate_hooks
    for j in range(SM_OFF // LANE):
        cs = slice(j * LANE, (j + 1) * LANE)
        for sq in range(bb):
            gv = gates[sq, :, cs]
            gact[sq, :, cs] = _silu(gv) if j * LANE < GC_OFF else _gelu_tanh(gv)
        fill()
    assert n_t == 1 or len(fillers) >= conv_pieces
    rate[0] = max(0, len(fillers) - out_n) / mid_hooks
    credit[0] = 0.0
    for sq in range(bb):
        sm = gates[sq, :, SM_OFF:SM_OFF + LANE]
        sp = _softplus(sm + misc[:, M_BIAS:M_BIAS + LANE])
        scal[sq, 0] = _sigmoid(sm)
        scal[sq, 1] = sp
        scal[sq, 2] = sp * nega

    ri = lax.broadcasted_iota(jnp.int32, (c, 2 * c), 0)
    cjw = lax.broadcasted_iota(jnp.int32, (c, 2 * c), 1)
    first_w = cjw < c
    cj = cjw % c
    causal_w = ri >= cj
    strict_w = ri > cj
    tril = jnp.where(lax.broadcasted_iota(jnp.int32, (c, c), 0) >= lax.broadcasted_iota(jnp.int32, (c, c), 1),
                     1.0, 0.0).astype(BF16)

    items = [(sq, ck) for sq in range(bb) for ck in range(nck)]
    rows = {it: slice(it[1] * c, (it[1] + 1) * c) for it in items}

    sc = {}
    for it in items:
        sq, rs = it[0], rows[it]
        g = _dot_left_exact(tril, scal[sq, 2, rs, :], 2)
        gt2 = jnp.concatenate([g, g], axis=0).T
        glast = g[c - 1:c, :]
        sc[it] = dict(g=g, gt2=gt2, eg=jnp.exp(g), egl=jnp.exp(glast - g), gtot=jnp.exp(glast))
        fill()

    def pairvec(arr, la_, lb_):
        return jnp.where(lo, arr[:, la_:la_ + 1], arr[:, lb_:lb_ + 1])

    def decay_w(d, la_, lb_):
        gcol = jnp.where(first_w, d['g'][:, la_:la_ + 1], d['g'][:, lb_:lb_ + 1])
        grow = jnp.where(first_w[0:1], d['gt2'][la_:la_ + 1, :], d['gt2'][lb_:lb_ + 1, :])
        return jnp.exp(jnp.where(causal_w, gcol - grow, -jnp.inf))

    def state_scale(d, la_, lb_):
        return jnp.where(row_lo, d['gtot'][:, la_:la_ + 1], d['gtot'][:, lb_:lb_ + 1])

    units = [(it, p) for it in items for p in range(N_PAIR)]

    ga = {}
    for un in units:
        (sq, _), p = un
        rs = rows[un[0]]
        qn = act[sq, rs, p * LANE:(p + 1) * LANE]
        kn = act[sq, rs, K_OFF + p * LANE:K_OFF + (p + 1) * LANE]
        ksplit = _head_split_rows(kn, lo).astype(BF16)
        kq = _dot_nt(jnp.concatenate([kn.astype(BF16), qn.astype(BF16)], axis=0), ksplit)
        ga[un] = dict(qn=qn, kn=kn, kq=kq)
        fill()
    lws = []
    for un in units:
        (sq, _), p = un
        rs, d, a = rows[un[0]], sc[un[0]], ga[un]
        ha, hb_ = 2 * p, 2 * p + 1
        gam = decay_w(d, ALPHA_L + ha, ALPHA_L + hb_)
        beta_c = scal[sq, 0, rs, :]
        bcol = jnp.where(first_w, beta_c[:, BETA_L + ha:BETA_L + ha + 1], beta_c[:, BETA_L + hb_:BETA_L + hb_ + 1])
        lws.append(jnp.where(strict_w, bcol * a['kq'][:c] * gam, 0.0))
        a['qk'] = (a['kq'][c:] * gam).astype(BF16)
        bvec = pairvec(beta_c, BETA_L + ha, BETA_L + hb_)
        egv = pairvec(d['eg'], ALPHA_L + ha, ALPHA_L + hb_)
        eglv = pairvec(d['egl'], ALPHA_L + ha, ALPHA_L + hb_)
        vp = act[sq, rs, V_OFF + p * LANE:V_OFF + (p + 1) * LANE]
        rhs = jnp.concatenate([bvec * vp, bvec * egv * a['kn']], axis=1)
        a['rhs'] = _head_split_rows(rhs, lo2).astype(BF16)
        a['qd'] = a['qn'] * egv
        a['kd'] = (a['kn'] * eglv).astype(BF16)
        del a['kq']
        fill()

    gb = {}
    for it in items:
        sq, rs = it[0], rows[it]
        bm = act[sq, rs, BM_OFF:BM_OFF + LANE]
        cm = act[sq, rs, CM_OFF:CM_OFF + LANE]
        bm_sw = pltpu.roll(bm, HEAD, 1)
        cm_sw = pltpu.roll(cm, HEAD, 1)
        for p in range(N_PAIR):
            ha, hb_ = 2 * p, 2 * p + 1
            g0, g1 = ha // (B_HEADS // B_GROUPS), hb_ // (B_HEADS // B_GROUPS)
            if g0 == g1 == 0:
                bsel, csel = jnp.where(lo, bm, bm_sw), jnp.where(lo, cm, cm_sw)
            elif g0 == g1 == 1:
                bsel, csel = jnp.where(lo, bm_sw, bm), jnp.where(lo, cm_sw, cm)
            else:
                bsel, csel = bm, cm
            gb[(it, p)] = dict(bsel=bsel, csel=csel,
                               scores=_dot_nt(csel.astype(BF16), _head_split_rows(bsel, lo).astype(BF16)))
            fill()
    for un in units:
        (sq, _), p = un
        rs, d, b = rows[un[0]], sc[un[0]], gb[un]
        ha, hb_ = 2 * p, 2 * p + 1
        xs = act[sq, rs, XBC_OFF + p * LANE:XBC_OFF + (p + 1) * LANE]
        xdt = xs * pairvec(scal[sq, 1, rs, :], DT_L + ha, DT_L + hb_)
        b['xb'] = xdt.astype(BF16)
        mw = (b['scores'] * decay_w(d, DT_L + ha, DT_L + hb_)).astype(BF16)
        b['y'] = _dot(mw, _head_split_rows(xdt, lo).astype(BF16))
        b['cd'] = (b['csel'] * pairvec(d['eg'], DT_L + ha, DT_L + hb_)).astype(BF16)
        b['bd'] = (b['bsel'] * pairvec(d['egl'], DT_L + ha, DT_L + hb_)).astype(BF16)
        b['xs'] = xs
        del b['scores'], b['csel'], b['bsel']
        fill()

    tws = _tri_inv_wide(lws, c, fill)
    for un, tw in zip(units, tws):
        ga[un]['uw'] = _dot(tw.astype(BF16), ga[un]['rhs'])

    for ck in range(nck):
        cur = [((sq, ck), p) for sq in range(bb) for p in range(N_PAIR)]
        s_old = {un: s_ref[un[0][0], un[1]] for un in cur}
        h_old = {un: hs_ref[un[0][0], un[1]] for un in cur}
        wq = {un: _dot(jnp.concatenate([ga[un]['uw'][:, LANE:].astype(BF16), ga[un]['qd'].astype(BF16)], axis=0),
                       s_old[un].astype(BF16)) for un in cur}
        fill()
        yi = {un: _dot(gb[un]['cd'], h_old[un].astype(BF16)) for un in cur}
        fill()
        for un in cur:
            (sq, _), p = un
            rs, d, a, b = rows[un[0]], sc[un[0]], ga[un], gb[un]
            ha, hb_ = 2 * p, 2 * p + 1
            delta = a['uw'][:, :LANE] - wq[un][:c]
            o = wq[un][c:] + _dot(a['qk'], _head_split_rows(delta, lo).astype(BF16))
            upd = _dot_tn(a['kd'], delta.astype(BF16))
            s_ref[sq, p] = state_scale(d, ALPHA_L + ha, ALPHA_L + hb_) * s_old[un] + jnp.where(same_head, upd, 0.0)
            obuf[sq, rs, p * LANE:(p + 1) * LANE] = o
            updh = _dot_tn(b['bd'], b['xb'])
            hs_ref[sq, p] = state_scale(d, DT_L + ha, DT_L + hb_) * h_old[un] + jnp.where(same_head, updh, 0.0)
            y = b['y'] + yi[un] + misc[:, M_DSKIP + p * LANE:M_DSKIP + (p + 1) * LANE] * b['xs']
            obuf[sq, rs, A_WIDTH + p * LANE:A_WIDTH + (p + 1) * LANE] = (
                y * gact[sq, rs, ZB_OFF + p * LANE:ZB_OFF + (p + 1) * LANE])
            fill()

    g_width = B_WIDTH // B_GROUPS
    assert B_GROUPS == 2
    g3r =lax.broadcasted_iota(jnp.int32, (B_WIDTH, LANE), 0) // g_width
    g3c = lax.broadcasted_iota(jnp.int32, (B_WIDTH, LANE), 1) // HEAD
    ones_group = jnp.where(g3r == g3c, 1.0, 0.0).astype(BF16)
    in_group0 = lax.broadcasted_iota(jnp.int32, (1, B_WIDTH), 1) < g_width
    rowt = lax.broadcasted_iota(jnp.int32, (tb, LANE), 0)
    rate[0] = len(fillers) / out_hooks
    credit[0] = 0.0
    for sq in range(bb):
        ms_ = slice(sq * tb, (sq + 1) * tb)
        for p in range(N_PAIR):
            o = obuf[sq, :, p * LANE:(p + 1) * LANE]
            ms = head_sum(o * o) * (1.0 / A_DV)
            on = o * lax.rsqrt(ms + EPS) * misc[:, M_NA:M_NA + LANE]
            mixo[ms_, p * LANE:(p + 1) * LANE] = (
                on * gact[sq, :, ZA_OFF + p * LANE:ZA_OFF + (p + 1) * LANE]).astype(BF16)
            fill()
        yb = obuf[sq, :, A_WIDTH:A_WIDTH + B_WIDTH]
        gs = _dot_right_exact(yb * yb, ones_group, 1)
        ms = jnp.where(in_group0, gs[:, 0:1], gs[:, HEAD:HEAD + 1]) * (1.0 / g_width)
        ob = yb * lax.rsqrt(ms + EPS) * misc[:, M_NB:M_NB + B_WIDTH]
        mixo[ms_, A_WIDTH:A_WIDTH + B_WIDTH] = ob.astype(BF16)
        fill()

        xc = act[sq, :, XC_OFF:XC_OFF + C_WIDTH]
        rig = _dot(xc.astype(BF16), wri_ref[...]) + misc[:, M_BRI:M_BRI + 2 * C_WIDTH]
        for half in range(C_WIDTH // LANE):
            hs_ = slice(half * LANE, (half + 1) * LANE)
            xch = xc[:, hs_]
            lam = misc[:, M_LAM + half * LANE:M_LAM + (half + 1) * LANE]
            log_a = -LRU_C * _sigmoid(rig[:, hs_]) * _softplus(-lam)
            a = jnp.exp(log_a)
            b = jnp.sqrt(1.0 - jnp.exp(2.0 * log_a)) * (
                _sigmoid(rig[:, C_WIDTH + half * LANE:C_WIDTH + (half + 1) * LANE]) * xch)
            d = 1
            while d < tb:
                keep = rowt >= d
                a_sh = jnp.where(keep, pltpu.roll(a, d, 0), 1.0)
                b_sh = jnp.where(keep, pltpu.roll(b, d, 0), 0.0)
                b = a * b_sh + b
                a = a * a_sh
                d *= 2
            hseq = a * l_ref[sq, :, hs_] + b
            l_ref[sq, :, hs_] = hseq[tb - 1:tb, :]
            mixo[ms_, A_WIDTH + B_WIDTH + half * LANE:A_WIDTH + B_WIDTH + (half + 1) * LANE] = (
                hseq * gact[sq, :, GC_OFF + half * LANE:GC_OFF + (half + 1) * LANE]).astype(BF16)
            fill()

    while fillers:
        fillers.pop(0)()
    mix = _dot(mixo[...], wout_ref[...])
    for sq in range(bb):
        xo_ref[sq] = x2c[sq] + mix[sq * tb:(sq + 1) * tb]


def _mixer(x, pv, win, wri, wout, n1, wg, wu, wd, s0, h0, l0, c0, *, chunk, tb, bb, layer):
    bsz, l, _ = x.shape
    assert l % tb == 0 and tb % chunk == 0 and bsz % bb == 0
    n_t = l // tb
    seq = lambda b, t: (b, 0, 0)
    seq4 = lambda b, t: (b, 0, 0, 0)
    in_specs = [
        pl.BlockSpec((bb, tb, D_MODEL), lambda b, t: (b, t, 0)),
        pl.BlockSpec((bb, tb, D_MODEL), lambda b, t: (b, jnp.minimum(t + 1, n_t - 1), 0)),
        _resident((PV_ROWS, CONV_CH)),
        _resident((D_MODEL, D_IN_R)),
        _resident((C_WIDTH, 2 * C_WIDTH)),
        _resident((D_MIX, D_MODEL), layer),
        _resident((1, D_MODEL)),
        _resident((D_MODEL, D_FF), layer),
        _resident((D_MODEL, D_FF), layer),
        _resident((D_FF, D_MODEL), layer),
        pl.BlockSpec((bb, N_PAIR, LANE, LANE), seq4),
        pl.BlockSpec((bb, N_PAIR, LANE, LANE), seq4),
        pl.BlockSpec((bb, 1, C_WIDTH), seq),
        pl.BlockSpec((bb, SUBLANE, CONV_CH), seq),
    ]
    out_specs = [
        pl.BlockSpec((bb, tb, D_MODEL), lambda b, t: (b, t, 0)),
        pl.BlockSpec((bb, N_PAIR, LANE, LANE), seq4),
        pl.BlockSpec((bb, N_PAIR, LANE, LANE), seq4),
        pl.BlockSpec((bb, 1, C_WIDTH), seq),
        pl.BlockSpec((bb, SUBLANE, CONV_CH), seq),
    ]
    out_shape = [
        jax.ShapeDtypeStruct((bsz, l, D_MODEL), F32),
        jax.ShapeDtypeStruct((bsz, N_PAIR, LANE, LANE), F32),
        jax.ShapeDtypeStruct((bsz, N_PAIR, LANE, LANE), F32),
        jax.ShapeDtypeStruct((bsz, 1, C_WIDTH), F32),
        jax.ShapeDtypeStruct((bsz, SUBLANE, CONV_CH), F32),
    ]
    scratch = [
        pltpu.VMEM((bb, tb + SUBLANE, CONV_CH), F32),
        pltpu.VMEM((bb, tb, CONV_CH), F32),
        pltpu.VMEM((bb, tb, GATE_CH), F32),
        pltpu.VMEM((bb, tb, SM_OFF), F32),
        pltpu.VMEM((bb, 3, tb, LANE), F32),
        pltpu.VMEM((bb, tb, A_WIDTH + B_WIDTH), F32),
        pltpu.VMEM((bb * tb, D_MIX), BF16),
        pltpu.VMEM((bb * tb, D_MODEL), BF16),
        pltpu.VMEM((bb * tb, D_MODEL), BF16),
        pltpu.VMEM((bb * tb, MXU_TILE), F32),
        pltpu.VMEM((2, bb * tb, MXU_TILE), BF16),
        pltpu.VMEM((bb * tb, D_MODEL), F32),
        pltpu.VMEM((bb, tb, D_MODEL), F32),
        pltpu.VMEM((bb, tb, D_MODEL), F32),
    ]
    return pl.pallas_call(
        functools.partial(_mixer_kernel, chunk=chunk, tb=tb, bb=bb, n_t=n_t),
        out_shape=out_shape,
        grid=(bsz // bb, l // tb),
        in_specs=in_specs,
        out_specs=out_specs,
        scratch_shapes=scratch,
        compiler_params=pltpu.CompilerParams(dimension_semantics=("arbitrary", "arbitrary"),
                                             vmem_limit_bytes=VMEM_LIMIT),
        name="mixer_c%d" % chunk,
    )(x, x, pv, win, wri, wout, n1, wg, wu, wd, s0, h0, l0, c0)


def _pair_states(s):
    bsz = s.shape[0]
    s = s.reshape(bsz, N_PAIR, 2, HEAD, HEAD)
    z = jnp.zeros((bsz, N_PAIR, HEAD, HEAD), s.dtype)
    top = jnp.concatenate([s[:, :, 0], z], axis=-1)
    bot = jnp.concatenate([z, s[:, :, 1]], axis=-1)
    return jnp.concatenate([top, bot], axis=-2)


def _unpair_states(sp):
    a = sp[:, :, :HEAD, :HEAD]
    b = sp[:, :, HEAD:, HEAD:]
    return jnp.stack([a, b], axis=2).reshape(sp.shape[0], A_HEADS, HEAD, HEAD)


W_ZA = A_CONV_CH
W_BA = W_ZA + A_WIDTH
W_AA = W_BA + A_HEADS
W_ZB = W_AA + A_HEADS
W_XBC = W_ZB + B_WIDTH
W_DT = W_XBC + B_CONV_CH
W_GC = W_DT + B_HEADS
W_XC = W_GC + C_WIDTH
D_IN = W_XC + C_WIDTH
assert W_BA % LANE == BETA_L and W_AA % LANE == ALPHA_L and W_DT % LANE == DT_L


def _reorder_kernel(w_ref, o_ref):
    def put(dst, src, size):
        o_ref[:, dst:dst + size] = w_ref[0, :, src:src + size].astype(BF16)

    put(0, 0, A_CONV_CH)
    put(XBC_OFF, W_XBC, B_CONV_CH)
    put(XC_OFF, W_XC, C_WIDTH)
    put(CONV_CH + ZA_OFF, W_ZA, A_WIDTH)
    put(CONV_CH + ZB_OFF, W_ZB, B_WIDTH)
    put(CONV_CH + GC_OFF, W_GC, C_WIDTH)
    lane = lax.broadcasted_iota(jnp.int32, (1, LANE), 1)
    ba = w_ref[0, :, W_BA - BETA_L:W_BA - BETA_L + LANE]
    dt = w_ref[0, :, W_DT - DT_L:W_DT - DT_L + LANE]
    small = jnp.where(lane < DT_L, ba, jnp.where(lane < DT_L + B_HEADS, dt, 0.0))
    o_ref[:, CONV_CH + SM_OFF:CONV_CH + SM_OFF + LANE] = small.astype(BF16)


def _reorder_w_in(w_in, layer, rows=256):
    return pl.pallas_call(
        _reorder_kernel,
        out_shape=jax.ShapeDtypeStruct((D_MODEL, D_IN_R), BF16),
        grid=(D_MODEL // rows,),
        in_specs=[pl.BlockSpec((1, rows, D_IN), lambda i: (layer, i, 0))],
        out_specs=pl.BlockSpec((rows, D_IN_R), lambda i: (i, 0)),
        name="reorder_w_in",
    )(w_in)


def _layer_params(lp, w_in, layer):
    win = _reorder_w_in(w_in, layer)

    def lane_block(*pieces):
        v = jnp.zeros((LANE,), F32)
        for off, val in pieces:
            v = lax.dynamic_update_slice(v, val.astype(F32), (off,))
        return v

    misc = jnp.concatenate([
        lane_block((ALPHA_L, lp['dt_bias_a']), (DT_L, lp['dt_bias_b'])),
        lane_block((ALPHA_L, lp['a_log_a']), (DT_L, lp['a_log_b'])),
        jnp.tile(lp['norm_a_w'], 2),
        jnp.repeat(lp['d_skip_b'], B_HEADDIM),
        lp['norm_b_w'],
        lp['lru_lambda'],
        lp['b_rgate'], lp['b_igate'],
    ])
    misc = jnp.pad(misc, (0, CONV_CH - misc.shape[0]))
    cw = jnp.concatenate([lp['conv_a_w'], lp['conv_b_w'], lp['conv_c_w']], axis=1)
    cb = jnp.concatenate([jnp.zeros((A_CONV_CH,), F32), lp['conv_b_b'], lp['conv_c_b']])
    pv = jnp.concatenate([jnp.pad(lp['norm_mix'], (0, CONV_CH - D_MODEL))[None], cw, cb[None], misc[None],
                          jnp.zeros((1, CONV_CH), F32)], axis=0)

    def block_diag(wb):
        eye = jnp.eye(C_BLOCKS, dtype=wb.dtype)
        return jnp.einsum('ncd,nm->ncmd', wb, eye).reshape(C_WIDTH, C_WIDTH)

    wri = jnp.concatenate([block_diag(lp['w_rgate']), block_diag(lp['w_igate'])], axis=1).astype(BF16)
    return dict(pv=pv, win=win, wri=wri, n1=lp['norm_ffn1'][None], n2=lp['norm_ffn2'][None])


def _trunk(x, states, layers, big, norm_final, *, chunk, tb, bb, tm):
    bsz, l, _ = x.shape
    delta_s, delta_conv, ssd_h, ssd_conv, lru_h, lru_conv = states
    outs = [[] for _ in range(6)]
    for layer in range(DEPTH):
        lw = layers[layer]
        conv0 = jnp.concatenate([delta_conv[layer], ssd_conv[layer], lru_conv[layer]], axis=-1).astype(F32)
        conv0 = jnp.pad(conv0, ((0, 0), (SUBLANE - (CONV_W - 1), 0), (0, 0)))
        x3, s_new, h_new, l_new, c_new = _mixer(
            x, lw['pv'], lw['win'], lw['wri'], big['wout'], lw['n1'], big['g1'], big['u1'], big['d1'],
            _pair_states(delta_s[layer].astype(F32)), _pair_states(ssd_h[layer].astype(F32)),
            lru_h[layer].astype(F32)[:, None, :], conv0, chunk=chunk, tb=tb, bb=bb, layer=layer)
        fin = norm_final[None] if layer == DEPTH - 1 else None
        x = _ffn(x3.reshape(bsz * l, D_MODEL), lw['n2'], big['g2'], big['u2'], big['d2'], fin,
                 tm=tm, layer=layer).reshape(bsz, l, D_MODEL)
        c_new = c_new[:, SUBLANE - (CONV_W - 1):, :]
        outs[0].append(_unpair_states(s_new))
        outs[1].append(c_new[..., :A_CONV_CH])
        outs[2].append(_unpair_states(h_new))
        outs[3].append(c_new[..., XBC_OFF:XBC_OFF + B_CONV_CH])
        outs[4].append(l_new[:, 0, :])
        outs[5].append(c_new[..., XC_OFF:])
    return x, tuple(jnp.stack(o) for o in outs)


def kernel(x_prompt, x_sample, state_delta_s, state_delta_conv, state_ssd_h, state_ssd_conv, state_lru_h, state_lru_conv, norm_ffn1, ffn1_w_gate, ffn1_w_up, ffn1_w_down, norm_mix, w_in, conv_a_w, a_log_a, dt_bias_a, norm_a_w, conv_b_w, conv_b_b, a_log_b, dt_bias_b, d_skip_b, norm_b_w, conv_c_w, conv_c_b, w_rgate, b_rgate, w_igate, b_igate, lru_lambda, w_out, norm_ffn2, ffn2_w_gate, ffn2_w_up, ffn2_w_down, norm_final):
    params = {
        'norm_ffn1': norm_ffn1, 'ffn1_w_gate': ffn1_w_gate, 'ffn1_w_up': ffn1_w_up, 'ffn1_w_down': ffn1_w_down,
        'norm_mix': norm_mix, 'w_in': w_in,
        'conv_a_w': conv_a_w, 'a_log_a': a_log_a, 'dt_bias_a': dt_bias_a, 'norm_a_w': norm_a_w,
        'conv_b_w': conv_b_w, 'conv_b_b': conv_b_b, 'a_log_b': a_log_b, 'dt_bias_b': dt_bias_b,
        'd_skip_b': d_skip_b, 'norm_b_w': norm_b_w,
        'conv_c_w': conv_c_w, 'conv_c_b': conv_c_b, 'w_rgate': w_rgate, 'b_rgate': b_rgate,
        'w_igate': w_igate, 'b_igate': b_igate, 'lru_lambda': lru_lambda,
        'w_out': w_out,
        'norm_ffn2': norm_ffn2, 'ffn2_w_gate': ffn2_w_gate, 'ffn2_w_up': ffn2_w_up, 'ffn2_w_down': ffn2_w_down,
    }
    matmul_weights = dict(wout='w_out', g1='ffn1_w_gate', u1='ffn1_w_up', d1='ffn1_w_down',
                          g2='ffn2_w_gate', u2='ffn2_w_up', d2='ffn2_w_down')
    big = {short: params[name].astype(BF16) for short, name in matmul_weights.items()}
    small = {k: v for k, v in params.items() if k != 'w_in' and k not in matmul_weights.values()}
    layers = [_layer_params({k: v[i] for k, v in small.items()}, w_in, i) for i in range(DEPTH)]
    bp = x_prompt.shape[0]
    zero_states = (
        jnp.zeros((DEPTH, bp, A_HEADS, A_DK, A_DV), F32),
        jnp.zeros((DEPTH, bp, CONV_W - 1, A_CONV_CH), F32),
        jnp.zeros((DEPTH, bp, B_HEADS, B_STATE, B_HEADDIM), F32),
        jnp.zeros((DEPTH, bp, CONV_W - 1, B_CONV_CH), F32),
        jnp.zeros((DEPTH, bp, C_WIDTH), F32),
        jnp.zeros((DEPTH, bp, CONV_W - 1, C_WIDTH), F32),
    )
    y_prompt, p_states = _trunk(x_prompt, zero_states, layers, big, norm_final, chunk=128, tb=256, bb=1, tm=512)
    sample_states = (state_delta_s, state_delta_conv, state_ssd_h, state_ssd_conv, state_lru_h, state_lru_conv)
    y_sample, s_states = _trunk(x_sample, sample_states, layers, big, norm_final, chunk=64, tb=64, bb=4, tm=512)
    return (y_prompt, y_sample) + tuple(p_states) + tuple(s_states)
```

```python
import functools

import jax
import jax.numpy as jnp
from jax import lax
from jax.experimental import pallas as pl
from jax.experimental.pallas import tpu as pltpu

F32 = jnp.float32
BF16 = jnp.bfloat16

D_MODEL = 1024
DEPTH = 2
D_FF = 2816
EPS = 1e-6
CONV_W = 4
A_HEADS = 6
A_DK = 64
A_DV = 64
A_WIDTH = A_HEADS * A_DV
A_CONV_CH = 2 * A_HEADS * A_DK + A_WIDTH
B_HEADS = 6
B_HEADDIM = 64
B_WIDTH = B_HEADS * B_HEADDIM
B_GROUPS = 2
B_STATE = 64
B_CONV_CH = B_WIDTH + 2 * B_GROUPS * B_STATE
C_WIDTH = 256
C_BLOCKS = 8
C_BLOCK = C_WIDTH // C_BLOCKS
LRU_C = 8.0
D_MIX = A_WIDTH + B_WIDTH + C_WIDTH

LANE = 128
SUBLANE = 8
MXU_TILE = 256
assert D_FF % MXU_TILE == 0
HEAD = 64
N_PAIR = A_HEADS // 2
INV_BASE = 16
OUT_STAGE_THUNKS = 10

CONV_CH = A_CONV_CH + B_CONV_CH + C_WIDTH
K_OFF = A_HEADS * A_DK
V_OFF = 2 * A_HEADS * A_DK
XBC_OFF = A_CONV_CH
BM_OFF = XBC_OFF + B_WIDTH
CM_OFF = BM_OFF + B_GROUPS * B_STATE
XC_OFF = A_CONV_CH + B_CONV_CH
GATE_CH = A_WIDTH + B_WIDTH + C_WIDTH + LANE
ZA_OFF, ZB_OFF, GC_OFF, SM_OFF = 0, A_WIDTH, A_WIDTH + B_WIDTH, A_WIDTH + B_WIDTH + C_WIDTH
D_IN_R = CONV_CH + GATE_CH
BETA_L, ALPHA_L, DT_L = 0, A_HEADS, 2 * A_HEADS
PV_ROWS = 8
PV_NORM, PV_CW, PV_CB, PV_MISC = 0, 1, 5, 6
M_BIAS, M_ALOG, M_NA, M_DSKIP, M_NB, M_LAM, M_BRI = 0, 128, 256, 384, 768, 1152, 1408

VMEM_LIMIT = 56 * 1024 * 1024


def _dot(a, b):
    return jnp.dot(a, b, preferred_element_type=F32)


def _dot_nt(a, b):
    return lax.dot_general(a, b, (((1,), (1,)), ((), ())), preferred_element_type=F32)


def _dot_tn(a, b):
    return lax.dot_general(a, b, (((0,), (0,)), ((), ())), preferred_element_type=F32)


def _split(x, n):
    parts = []
    r = x
    for _ in range(n):
        p = r.astype(BF16)
        parts.append(p)
        r = r - p.astype(F32)
    return parts


def _dot_left_exact(m, x, n):
    return sum(_dot(m, p) for p in _split(x, n))


def _dot_right_exact(x, m, n):
    return sum(_dot(p, m) for p in _split(x, n))


def _sigmoid(x):
    return 0.5 + 0.5 * jnp.tanh(0.5 * x)


def _silu(x):
    h = 0.5 * x
    return h + h * jnp.tanh(h)


def _softplus(x):
    return jnp.maximum(x, 0.0) + jnp.log1p(jnp.exp(-jnp.abs(x)))


def _gelu_tanh(x):
    return 0.5 * x * (1.0 + jnp.tanh(0.7978845608028654 * (x + 0.044715 * x * x * x)))


def _rms(x, w):
    ms = jnp.mean(x * x, axis=-1, keepdims=True)
    return x * lax.rsqrt(ms + EPS) * w


def _ffn_body(x_ref, nw_ref, wg_ref, wu_ref, wd_ref, fnw_ref, o_ref, n_split):
    x = x_ref[...]
    h = _rms(x, nw_ref[...]).astype(BF16)
    tiles = D_FF // MXU_TILE
    bounds = [MXU_TILE * ((tiles * j + n_split - 1) // n_split) for j in range(n_split)] + [D_FF]
    acc = None
    for lo_, hi_ in zip(bounds[:-1], bounds[1:]):
        g = _dot(h, wg_ref[:, lo_:hi_])
        u = _dot(h, wu_ref[:, lo_:hi_])
        a = (_silu(g) * u).astype(BF16)
        d = _dot(a, wd_ref[lo_:hi_, :])
        acc = d if acc is None else acc + d
    y = x + 0.5 * acc
    if fnw_ref is not None:
        y = _rms(y, fnw_ref[...])
    o_ref[...] = y


def _ffn_kernel(x_ref, nw_ref, wg_ref, wu_ref, wd_ref, o_ref, *, n_split):
    _ffn_body(x_ref, nw_ref, wg_ref, wu_ref, wd_ref, None, o_ref, n_split)


def _ffn_final_kernel(x_ref, nw_ref, wg_ref, wu_ref, wd_ref, fnw_ref, o_ref, *, n_split):
    _ffn_body(x_ref, nw_ref, wg_ref, wu_ref, wd_ref, fnw_ref, o_ref, n_split)


def _resident(shape, layer=None):
    if layer is None:
        return pl.BlockSpec(shape, lambda *_: (0,) * len(shape), pipeline_mode=pl.Buffered(1))
    return pl.BlockSpec((None,) + tuple(shape), lambda *_: (layer,) + (0,) * len(shape),
                        pipeline_mode=pl.Buffered(1))


def _ffn(x2d, nw, wg, wu, wd, final_w=None, *, tm, layer):
    t = x2d.shape[0]
    assert t % tm == 0
    row = pl.BlockSpec((tm, D_MODEL), lambda i: (i, 0))
    in_specs = [row, _resident((1, D_MODEL)), _resident((D_MODEL, D_FF), layer), _resident((D_MODEL, D_FF), layer),
                _resident((D_FF, D_MODEL), layer)]
    args = [x2d, nw, wg, wu, wd]
    if final_w is None:
        body = functools.partial(_ffn_kernel, n_split=2)
    else:
        body = functools.partial(_ffn_final_kernel, n_split=2)
        in_specs.append(_resident((1, D_MODEL)))
        args.append(final_w)
    return pl.pallas_call(
        body,
        out_shape=jax.ShapeDtypeStruct((t, D_MODEL), F32),
        grid=(t // tm,),
        in_specs=in_specs,
        out_specs=row,
        compiler_params=pltpu.CompilerParams(dimension_semantics=("arbitrary",), vmem_limit_bytes=VMEM_LIMIT),
        name="ffn_final" if final_w is not None else "ffn",
    )(*args)


def _wide_blockdiag(xw, c):
    if c % LANE == 0:
        z = jnp.zeros((c, c), BF16)
        xb = xw.astype(BF16)
        return jnp.concatenate([jnp.concatenate([xb[:, :c], z], axis=1),
                                jnp.concatenate([z, xb[:, c:]], axis=1)], axis=0)
    first = lax.broadcasted_iota(jnp.int32, (1, 2 * c), 1) < c
    return jnp.concatenate([jnp.where(first, xw, 0.0).astype(BF16), jnp.where(first, 0.0, xw).astype(BF16)], axis=0)


def _head_split_rows(x, lo):
    return jnp.concatenate([jnp.where(lo, x, 0.0), jnp.where(lo, 0.0, x)], axis=0)


def _tri_inv_wide(lws, c, between):
    w = 2 * c
    nb = c // INV_BASE
    ri = lax.broadcasted_iota(jnp.int32, (c, w), 0)
    cj = lax.broadcasted_iota(jnp.int32, (c, w), 1) % c
    lane_w = lax.broadcasted_iota(jnp.int32, (1, w), 1)
    lane_blk = (lane_w % c) // INV_BASE
    blk_rows = [jnp.where(lane_w // INV_BASE == kb, 1.0, 0.0).astype(BF16) for kb in range(2 * nb)]

    def to_slab(x):
        return sum(jnp.where(lane_blk == b, x[b * INV_BASE:(b + 1) * INV_BASE, :], 0.0) for b in range(nb))

    def slab_blockdiag(sl):
        sb = sl.astype(BF16)
        return jnp.concatenate([sb * m for m in blk_rows], axis=0)

    def from_slab(sl):
        return jnp.concatenate([jnp.where(lane_blk == b, sl, 0.0) for b in range(nb)], axis=0)

    eye_slab = jnp.where(lax.broadcasted_iota(jnp.int32, (INV_BASE, w), 0) == lane_w % INV_BASE, 1.0, 0.0)
    ns = [-to_slab(lw) for lw in lws]
    ts = [eye_slab + n for n in ns]
    ps = [_dot(n.astype(BF16), slab_blockdiag(n)) for n in ns]
    between()
    n_pow = 2
    while 2 * n_pow < INV_BASE:
        outs = [_dot(jnp.concatenate([p.astype(BF16), t.astype(BF16)], axis=0), slab_blockdiag(p))
                for p, t in zip(ps, ts)]
        between()
        ts = [t + o[INV_BASE:] for t, o in zip(ts, outs)]
        ps = [o[:INV_BASE] for o in outs]
        n_pow *= 2
    ts = [t + _dot(t.astype(BF16), slab_blockdiag(p)) for t, p in zip(ts, ps)]
    between()
    ts = [from_slab(t) for t in ts]

    def mm(a, bw):
        return _dot(a.astype(BF16), _wide_blockdiag(bw, c))

    s = INV_BASE
    while s < c:
        pairs = range(c // (2 * s))
        odd = lambda x: jnp.concatenate([x[(2 * i + 1) * s:(2 * i + 2) * s] for i in pairs], axis=0)
        sub = (ri // (2 * s) == cj // (2 * s)) & ((ri // s) % 2 == 1) & ((cj // s) % 2 == 0)
        ets = [mm(odd(jnp.where(sub, lw, 0.0)), t) for lw, t in zip(lws, ts)]
        between()
        zero = jnp.zeros((s, w), F32)
        spread = lambda x: jnp.concatenate([y for i in pairs for y in (zero, x[i * s:(i + 1) * s])], axis=0)
        news = [odd(t) - mm(odd(t), spread(et)) for t, et in zip(ts, ets)]
        between()
        ts = [jnp.concatenate([y for i in pairs for y in (t[2 * i * s:(2 * i + 1) * s], nw[i * s:(i + 1) * s])],
                              axis=0) for t, nw in zip(ts, news)]
        s *= 2
    return ts


def _mixer_kernel(x_ref, xn_ref, pv_ref, win_ref, wri_ref, wout_ref, n1_ref, wg_ref, wu_ref, wd_ref,
                  s0_ref, h0_ref, l0_ref, c0_ref,
                  xo_ref, s_ref, hs_ref, l_ref, co_ref,
                  cbuf, act, gates, gact, scal, obuf, mixo, hbuf, hbuf1, gbuf, abuf, facc, x2c, x2n,
                  *, chunk, tb, bb, n_t):
    c = chunk
    nck = tb // c
    t_idx = pl.program_id(1)
    n_piece = pl.cdiv(D_IN_R, MXU_TILE)
    conv_pieces = CONV_CH // MXU_TILE
    ff_tiles = D_FF // MXU_TILE

    def front_pieces(src_ref, x2dst):
        def norm1():
            for sq in range(bb):
                hbuf1[sq * tb:(sq + 1) * tb, :] = _rms(src_ref[sq], n1_ref[...]).astype(BF16)

        def gate(j):
            gbuf[...] = _dot(hbuf1[...], wg_ref[:, j * MXU_TILE:(j + 1) * MXU_TILE])

        def up(j):
            u = _dot(hbuf1[...], wu_ref[:, j * MXU_TILE:(j + 1) * MXU_TILE])
            abuf[j % 2] = (_silu(gbuf[...]) * u).astype(BF16)

        def down(j):
            d = _dot(abuf[j % 2], wd_ref[j * MXU_TILE:(j + 1) * MXU_TILE, :])
            facc[...] = d if j == 0 else facc[...] + d

        def norm2():
            for sq in range(bb):
                x2 = src_ref[sq] + 0.5 * facc[sq * tb:(sq + 1) * tb, :]
                x2dst[sq] = x2
                hbuf[sq * tb:(sq + 1) * tb, :] = _rms(x2, pv_ref[PV_NORM:PV_NORM + 1, 0:D_MODEL]).astype(BF16)

        def piece(j):
            lo_, hi_ = j * MXU_TILE, min((j + 1) * MXU_TILE, D_IN_R)
            res = _dot(hbuf[...], win_ref[:, lo_:hi_])
            for sq in range(bb):
                part = res[sq * tb:(sq + 1) * tb]
                if j < conv_pieces:
                    cbuf[sq, SUBLANE:SUBLANE + tb, lo_:hi_] = part
                else:
                    gates[sq, :, lo_ - CONV_CH:hi_ - CONV_CH] = part

        thunks = [norm1]
        for j in range(ff_tiles):
            thunks += [functools.partial(gate, j), functools.partial(up, j)]
            if j > 0:
                thunks.append(functools.partial(down, j - 1))
        thunks += [functools.partial(down, ff_tiles - 1), norm2]
        return (thunks + [functools.partial(piece, j) for j in range(conv_pieces)],
                [functools.partial(piece, j) for j in range(conv_pieces, n_piece)])

    @pl.when(t_idx == 0)
    def _():
        s_ref[...] = s0_ref[...]
        hs_ref[...] = h0_ref[...]
        l_ref[...] = l0_ref[...]
        cbuf[:, 0:SUBLANE, :] = c0_ref[...]
        for thunk in front_pieces(x_ref, x2c)[0]:
            thunk()

    if n_t > 1:
        @pl.when(t_idx > 0)
        def _():
            x2c[...] = x2n[...]

    ahead, gate_pieces = front_pieces(xn_ref, x2n)
    fillers = gate_pieces + (ahead if n_t > 1 else [])
    n_units = bb * nck * N_PAIR
    conv_hooks, gate_hooks, out_hooks = CONV_CH // LANE, SM_OFF // LANE, 6 * bb
    mid_hooks = (bb * nck + 4 * n_units + 4 + 2 * max(0, (c // INV_BASE).bit_length() - 1)
                 + nck * (2 + bb * N_PAIR))
    assert len(gate_pieces) <= conv_hooks
    left = max(0, len(fillers) - conv_hooks)
    out_n = min(OUT_STAGE_THUNKS, left)
    gate_n = min(gate_hooks // 2, left - out_n)
    rate = [1.0]
    credit = [0.0]

    def fill():
        credit[0] += rate[0]
        while credit[0] >= 1.0 and fillers:
            fillers.pop(0)()
            credit[0] -= 1.0
    misc = pv_ref[PV_MISC:PV_MISC + 1, :]
    lane1 = lax.broadcasted_iota(jnp.int32, (1, LANE), 1)
    lo = lane1 < HEAD
    lo2 = (lax.broadcasted_iota(jnp.int32, (1, 2 * LANE), 1) % LANE) < HEAD
    dec_lanes = (lane1 >= ALPHA_L) & (lane1 < DT_L + B_HEADS)
    nega = jnp.where(dec_lanes, -jnp.exp(misc[:, M_ALOG:M_ALOG + LANE]), 0.0)
    r128 = lax.broadcasted_iota(jnp.int32, (LANE, LANE), 0)
    c128 = lax.broadcasted_iota(jnp.int32, (LANE, LANE), 1)
    same_head = (r128 // HEAD) == (c128 // HEAD)
    ones_head = jnp.where(same_head, 1.0, 0.0).astype(BF16)
    row_lo = r128 < HEAD

    def head_sum(v):
        return _dot_right_exact(v, ones_head, 1)

    row8 = lax.broadcasted_iota(jnp.int32, (SUBLANE, LANE), 0)
    tiles_per_piece = MXU_TILE // LANE
    for j in range(CONV_CH // LANE):
        cs = slice(j * LANE, (j + 1) * LANE)
        for sq in range(bb):
            xr = cbuf[sq, SUBLANE:SUBLANE + tb, cs]
            head = cbuf[sq, 0:SUBLANE, cs]
            acc = xr * pv_ref[PV_CW + CONV_W - 1:PV_CW + CONV_W, cs]
            for d in range(1, CONV_W):
                rolled = pltpu.roll(xr, d, 0)
                top = jnp.where(row8 < d, pltpu.roll(head, d, 0), rolled[0:SUBLANE])
                shifted = jnp.concatenate([top, rolled[SUBLANE:]], axis=0)
                acc = acc + shifted * pv_ref[PV_CW + CONV_W - 1 - d:PV_CW + CONV_W - d, cs]
            if j * LANE >= XBC_OFF:
                acc = acc + pv_ref[PV_CB:PV_CB + 1, cs]
            if j * LANE < XC_OFF:
                acc = _silu(acc)
            if j * LANE < V_OFF:
                acc = acc * lax.rsqrt(head_sum(acc * acc) + EPS)
                if j * LANE < K_OFF:
                    acc = acc * (A_DK ** -0.5)
            act[sq, :, cs] = acc
        if j % tiles_per_piece == tiles_per_piece - 1:
            ps_ = slice((j + 1 - tiles_per_piece) * LANE, (j + 1) * LANE)
            for sq in range(bb):
                tail = cbuf[sq, tb:tb + SUBLANE, ps_]
                co_ref[sq, :, ps_] = tail
                cbuf[sq, 0:SUBLANE, ps_] = tail
        fill()
    assert len(fillers) <= (len(ahead) if n_t > 1 else 0)
    rate[0] = gate_n / gate_hooks
    for j in range(SM_OFF // LANE):
        cs = slice(j * LANE, (j + 1) * LANE)
        for sq in range(bb):
            gv = gates[sq, :, cs]
            gact[sq, :, cs] = _silu(gv) if j * LANE < GC_OFF else _gelu_tanh(gv)
        fill()
    assert n_t == 1 or len(fillers) >= conv_pieces
    rate[0] = max(0, len(fillers) - out_n) / mid_hooks
    credit[0] = 0.0
    for sq in range(bb):
        sm = gates[sq, :, SM_OFF:SM_OFF + LANE]
        sp = _softplus(sm + misc[:, M_BIAS:M_BIAS + LANE])
        scal[sq, 0] = _sigmoid(sm)
        scal[sq, 1] = sp
        scal[sq, 2] = sp * nega

    ri = lax.broadcasted_iota(jnp.int32, (c, 2 * c), 0)
    cjw = lax.broadcasted_iota(jnp.int32, (c, 2 * c), 1)
    first_w = cjw < c
    cj = cjw % c
    causal_w = ri >= cj
    strict_w = ri > cj
    tril = jnp.where(lax.broadcasted_iota(jnp.int32, (c, c), 0) >= lax.broadcasted_iota(jnp.int32, (c, c), 1),
                     1.0, 0.0).astype(BF16)

    items = [(sq, ck) for sq in range(bb) for ck in range(nck)]
    rows = {it: slice(it[1] * c, (it[1] + 1) * c) for it in items}

    sc = {}
    for it in items:
        sq, rs = it[0], rows[it]
        g = _dot_left_exact(tril, scal[sq, 2, rs, :], 2)
        gt2 = jnp.concatenate([g, g], axis=0).T
        glast = g[c - 1:c, :]
        sc[it] = dict(g=g, gt2=gt2, eg=jnp.exp(g), egl=jnp.exp(glast - g), gtot=jnp.exp(glast))
        fill()

    def pairvec(arr, la_, lb_):
        return jnp.where(lo, arr[:, la_:la_ + 1], arr[:, lb_:lb_ + 1])

    def decay_w(d, la_, lb_):
        gcol = jnp.where(first_w, d['g'][:, la_:la_ + 1], d['g'][:, lb_:lb_ + 1])
        grow = jnp.where(first_w[0:1], d['gt2'][la_:la_ + 1, :], d['gt2'][lb_:lb_ + 1, :])
        return jnp.exp(jnp.where(causal_w, gcol - grow, -jnp.inf))

    def state_scale(d, la_, lb_):
        return jnp.where(row_lo, d['gtot'][:, la_:la_ + 1], d['gtot'][:, lb_:lb_ + 1])

    units = [(it, p) for it in items for p in range(N_PAIR)]

    ga = {}
    for un in units:
        (sq, _), p = un
        rs = rows[un[0]]
        qn = act[sq, rs, p * LANE:(p + 1) * LANE]
        kn = act[sq, rs, K_OFF + p * LANE:K_OFF + (p + 1) * LANE]
        ksplit = _head_split_rows(kn, lo).astype(BF16)
        kq = _dot_nt(jnp.concatenate([kn.astype(BF16), qn.astype(BF16)], axis=0), ksplit)
        ga[un] = dict(qn=qn, kn=kn, kq=kq)
        fill()
    lws = []
    for un in units:
        (sq, _), p = un
        rs, d, a = rows[un[0]], sc[un[0]], ga[un]
        ha, hb_ = 2 * p, 2 * p + 1
        gam = decay_w(d, ALPHA_L + ha, ALPHA_L + hb_)
        beta_c = scal[sq, 0, rs, :]
        bcol = jnp.where(first_w, beta_c[:, BETA_L + ha:BETA_L + ha + 1], beta_c[:, BETA_L + hb_:BETA_L + hb_ + 1])
        lws.append(jnp.where(strict_w, bcol * a['kq'][:c] * gam, 0.0))
        a['qk'] = (a['kq'][c:] * gam).astype(BF16)
        bvec = pairvec(beta_c, BETA_L + ha, BETA_L + hb_)
        egv = pairvec(d['eg'], ALPHA_L + ha, ALPHA_L + hb_)
        eglv = pairvec(d['egl'], ALPHA_L + ha, ALPHA_L + hb_)
        vp = act[sq, rs, V_OFF + p * LANE:V_OFF + (p + 1) * LANE]
        rhs = jnp.concatenate([bvec * vp, bvec * egv * a['kn']], axis=1)
        a['rhs'] = _head_split_rows(rhs, lo2).astype(BF16)
        a['qd'] = a['qn'] * egv
        a['kd'] = (a['kn'] * eglv).astype(BF16)
        del a['kq']
        fill()

    gb = {}
    for it in items:
        sq, rs = it[0], rows[it]
        bm = act[sq, rs, BM_OFF:BM_OFF + LANE]
        cm = act[sq, rs, CM_OFF:CM_OFF + LANE]
        bm_sw = pltpu.roll(bm, HEAD, 1)
        cm_sw = pltpu.roll(cm, HEAD, 1)
        for p in range(N_PAIR):
            ha, hb_ = 2 * p, 2 * p + 1
            g0, g1 = ha // (B_HEADS // B_GROUPS), hb_ // (B_HEADS // B_GROUPS)
            if g0 == g1 == 0:
                bsel, csel = jnp.where(lo, bm, bm_sw), jnp.where(lo, cm, cm_sw)
            elif g0 == g1 == 1:
                bsel, csel = jnp.where(lo, bm_sw, bm), jnp.where(lo, cm_sw, cm)
            else:
                bsel, csel = bm, cm
            gb[(it, p)] = dict(bsel=bsel, csel=csel,
                               scores=_dot_nt(csel.astype(BF16), _head_split_rows(bsel, lo).astype(BF16)))
            fill()
    for un in units:
        (sq, _), p = un
        rs, d, b = rows[un[0]], sc[un[0]], gb[un]
        ha, hb_ = 2 * p, 2 * p + 1
        xs = act[sq, rs, XBC_OFF + p * LANE:XBC_OFF + (p + 1) * LANE]
        xdt = xs * pairvec(scal[sq, 1, rs, :], DT_L + ha, DT_L + hb_)
        b['xb'] = xdt.astype(BF16)
        mw = (b['scores'] * decay_w(d, DT_L + ha, DT_L + hb_)).astype(BF16)
        b['y'] = _dot(mw, _head_split_rows(xdt, lo).astype(BF16))
        b['cd'] = (b['csel'] * pairvec(d['eg'], DT_L + ha, DT_L + hb_)).astype(BF16)
        b['bd'] = (b['bsel'] * pairvec(d['egl'], DT_L + ha, DT_L + hb_)).astype(BF16)
        b['xs'] = xs
        del b['scores'], b['csel'], b['bsel']
        fill()

    tws = _tri_inv_wide(lws, c, fill)
    for un, tw in zip(units, tws):
        ga[un]['uw'] = _dot(tw.astype(BF16), ga[un]['rhs'])

    for ck in range(nck):
        cur = [((sq, ck), p) for sq in range(bb) for p in range(N_PAIR)]
        s_old = {un: s_ref[un[0][0], un[1]] for un in cur}
        h_old = {un: hs_ref[un[0][0], un[1]] for un in cur}
        wq = {un: _dot(jnp.concatenate([ga[un]['uw'][:, LANE:].astype(BF16), ga[un]['qd'].astype(BF16)], axis=0),
                       s_old[un].astype(BF16)) for un in cur}
        fill()
        yi = {un: _dot(gb[un]['cd'], h_old[un].astype(BF16)) for un in cur}
        fill()
        for un in cur:
            (sq, _), p = un
            rs, d, a, b = rows[un[0]], sc[un[0]], ga[un], gb[un]
            ha, hb_ = 2 * p, 2 * p + 1
            delta = a['uw'][:, :LANE] - wq[un][:c]
            o = wq[un][c:] + _dot(a['qk'], _head_split_rows(delta, lo).astype(BF16))
            upd = _dot_tn(a['kd'], delta.astype(BF16))
            s_ref[sq, p] = state_scale(d, ALPHA_L + ha, ALPHA_L + hb_) * s_old[un] + jnp.where(same_head, upd, 0.0)
            obuf[sq, rs, p * LANE:(p + 1) * LANE] = o
            updh = _dot_tn(b['bd'], b['xb'])
            hs_ref[sq, p] = state_scale(d, DT_L + ha, DT_L + hb_) * h_old[un] + jnp.where(same_head, updh, 0.0)
            y = b['y'] + yi[un] + misc[:, M_DSKIP + p * LANE:M_DSKIP + (p + 1) * LANE] * b['xs']
            obuf[sq, rs, A_WIDTH + p * LANE:A_WIDTH + (p + 1) * LANE] = (
                y * gact[sq, rs, ZB_OFF + p * LANE:ZB_OFF + (p + 1) * LANE])
            fill()

    g_width = B_WIDTH // B_GROUPS
    assert B_GROUPS == 2
    g3r =lax.broadcasted_iota(jnp.int32, (B_WIDTH, LANE), 0) // g_width
    g3c = lax.broadcasted_iota(jnp.int32, (B_WIDTH, LANE), 1) // HEAD
    ones_group = jnp.where(g3r == g3c, 1.0, 0.0).astype(BF16)
    in_group0 = lax.broadcasted_iota(jnp.int32, (1, B_WIDTH), 1) < g_width
    rowt = lax.broadcasted_iota(jnp.int32, (tb, LANE), 0)
    rate[0] = len(fillers) / out_hooks
    credit[0] = 0.0
    for sq in range(bb):
        ms_ = slice(sq * tb, (sq + 1) * tb)
        for p in range(N_PAIR):
            o = obuf[sq, :, p * LANE:(p + 1) * LANE]
            ms = head_sum(o * o) * (1.0 / A_DV)
            on = o * lax.rsqrt(ms + EPS) * misc[:, M_NA:M_NA + LANE]
            mixo[ms_, p * LANE:(p + 1) * LANE] = (
                on * gact[sq, :, ZA_OFF + p * LANE:ZA_OFF + (p + 1) * LANE]).astype(BF16)
            fill()
        yb = obuf[sq, :, A_WIDTH:A_WIDTH + B_WIDTH]
        gs = _dot_right_exact(yb * yb, ones_group, 1)
        ms = jnp.where(in_group0, gs[:, 0:1], gs[:, HEAD:HEAD + 1]) * (1.0 / g_width)
        ob = yb * lax.rsqrt(ms + EPS) * misc[:, M_NB:M_NB + B_WIDTH]
        mixo[ms_, A_WIDTH:A_WIDTH + B_WIDTH] = ob.astype(BF16)
        fill()

        xc = act[sq, :, XC_OFF:XC_OFF + C_WIDTH]
        rig = _dot(xc.astype(BF16), wri_ref[...]) + misc[:, M_BRI:M_BRI + 2 * C_WIDTH]
        for half in range(C_WIDTH // LANE):
            hs_ = slice(half * LANE, (half + 1) * LANE)
            xch = xc[:, hs_]
            lam = misc[:, M_LAM + half * LANE:M_LAM + (half + 1) * LANE]
            log_a = -LRU_C * _sigmoid(rig[:, hs_]) * _softplus(-lam)
            a = jnp.exp(log_a)
            b = jnp.sqrt(1.0 - jnp.exp(2.0 * log_a)) * (
                _sigmoid(rig[:, C_WIDTH + half * LANE:C_WIDTH + (half + 1) * LANE]) * xch)
            d = 1
            while d < SUBLANE:
                keep = rowt % SUBLANE >= d
                a_sh = jnp.where(keep, pltpu.roll(a, d, 0), 1.0)
                b_sh = jnp.where(keep, pltpu.roll(b, d, 0), 0.0)
                b = a * b_sh + b
                a = a * a_sh
                d *= 2
            carry = l_ref[sq, :, hs_]
            tiles = []
            for v in range(tb // SUBLANE):
                vs = slice(v * SUBLANE, (v + 1) * SUBLANE)
                tiles.append(a[vs] * carry + b[vs])
                carry = tiles[-1][SUBLANE - 1:SUBLANE, :]
            hseq = jnp.concatenate(tiles, axis=0)
            l_ref[sq, :, hs_] = carry
            mixo[ms_, A_WIDTH + B_WIDTH + half * LANE:A_WIDTH + B_WIDTH + (half + 1) * LANE] = (
                hseq * gact[sq, :, GC_OFF + half * LANE:GC_OFF + (half + 1) * LANE]).astype(BF16)
            fill()

    while fillers:
        fillers.pop(0)()
    mix = _dot(mixo[...], wout_ref[...])
    for sq in range(bb):
        xo_ref[sq] = x2c[sq] + mix[sq * tb:(sq + 1) * tb]


def _mixer(x, pv, win, wri, wout, n1, wg, wu, wd, s0, h0, l0, c0, *, chunk, tb, bb, layer):
    bsz, l, _ = x.shape
    assert l % tb == 0 and tb % chunk == 0 and bsz % bb == 0
    n_t = l // tb
    seq = lambda b, t: (b, 0, 0)
    seq4 = lambda b, t: (b, 0, 0, 0)
    in_specs = [
        pl.BlockSpec((bb, tb, D_MODEL), lambda b, t: (b, t, 0)),
        pl.BlockSpec((bb, tb, D_MODEL), lambda b, t: (b, jnp.minimum(t + 1, n_t - 1), 0)),
        _resident((PV_ROWS, CONV_CH)),
        _resident((D_MODEL, D_IN_R)),
        _resident((C_WIDTH, 2 * C_WIDTH)),
        _resident((D_MIX, D_MODEL), layer),
        _resident((1, D_MODEL)),
        _resident((D_MODEL, D_FF), layer),
        _resident((D_MODEL, D_FF), layer),
        _resident((D_FF, D_MODEL), layer),
        pl.BlockSpec((bb, N_PAIR, LANE, LANE), seq4),
        pl.BlockSpec((bb, N_PAIR, LANE, LANE), seq4),
        pl.BlockSpec((bb, 1, C_WIDTH), seq),
        pl.BlockSpec((bb, SUBLANE, CONV_CH), seq),
    ]
    out_specs = [
        pl.BlockSpec((bb, tb, D_MODEL), lambda b, t: (b, t, 0)),
        pl.BlockSpec((bb, N_PAIR, LANE, LANE), seq4),
        pl.BlockSpec((bb, N_PAIR, LANE, LANE), seq4),
        pl.BlockSpec((bb, 1, C_WIDTH), seq),
        pl.BlockSpec((bb, SUBLANE, CONV_CH), seq),
    ]
    out_shape = [
        jax.ShapeDtypeStruct((bsz, l, D_MODEL), F32),
        jax.ShapeDtypeStruct((bsz, N_PAIR, LANE, LANE), F32),
        jax.ShapeDtypeStruct((bsz, N_PAIR, LANE, LANE), F32),
        jax.ShapeDtypeStruct((bsz, 1, C_WIDTH), F32),
        jax.ShapeDtypeStruct((bsz, SUBLANE, CONV_CH), F32),
    ]
    scratch = [
        pltpu.VMEM((bb, tb + SUBLANE, CONV_CH), F32),
        pltpu.VMEM((bb, tb, CONV_CH), F32),
        pltpu.VMEM((bb, tb, GATE_CH), F32),
        pltpu.VMEM((bb, tb, SM_OFF), F32),
        pltpu.VMEM((bb, 3, tb, LANE), F32),
        pltpu.VMEM((bb, tb, A_WIDTH + B_WIDTH), F32),
        pltpu.VMEM((bb * tb, D_MIX), BF16),
        pltpu.VMEM((bb * tb, D_MODEL), BF16),
        pltpu.VMEM((bb * tb, D_MODEL), BF16),
        pltpu.VMEM((bb * tb, MXU_TILE), F32),
        pltpu.VMEM((2, bb * tb, MXU_TILE), BF16),
        pltpu.VMEM((bb * tb, D_MODEL), F32),
        pltpu.VMEM((bb, tb, D_MODEL), F32),
        pltpu.VMEM((bb, tb, D_MODEL), F32),
    ]
    return pl.pallas_call(
        functools.partial(_mixer_kernel, chunk=chunk, tb=tb, bb=bb, n_t=n_t),
        out_shape=out_shape,
        grid=(bsz // bb, l // tb),
        in_specs=in_specs,
        out_specs=out_specs,
        scratch_shapes=scratch,
        compiler_params=pltpu.CompilerParams(dimension_semantics=("arbitrary", "arbitrary"),
                                             vmem_limit_bytes=VMEM_LIMIT),
        name="mixer_c%d" % chunk,
    )(x, x, pv, win, wri, wout, n1, wg, wu, wd, s0, h0, l0, c0)


def _pair_states(s):
    bsz = s.shape[0]
    s = s.reshape(bsz, N_PAIR, 2, HEAD, HEAD)
    z = jnp.zeros((bsz, N_PAIR, HEAD, HEAD), s.dtype)
    top = jnp.concatenate([s[:, :, 0], z], axis=-1)
    bot = jnp.concatenate([z, s[:, :, 1]], axis=-1)
    return jnp.concatenate([top, bot], axis=-2)


def _unpair_states(sp):
    a = sp[:, :, :HEAD, :HEAD]
    b = sp[:, :, HEAD:, HEAD:]
    return jnp.stack([a, b], axis=2).reshape(sp.shape[0], A_HEADS, HEAD, HEAD)


W_ZA = A_CONV_CH
W_BA = W_ZA + A_WIDTH
W_AA = W_BA + A_HEADS
W_ZB = W_AA + A_HEADS
W_XBC = W_ZB + B_WIDTH
W_DT = W_XBC + B_CONV_CH
W_GC = W_DT + B_HEADS
W_XC = W_GC + C_WIDTH
D_IN = W_XC + C_WIDTH
assert W_BA % LANE == BETA_L and W_AA % LANE == ALPHA_L and W_DT % LANE == DT_L


def _reorder_kernel(w_ref, o_ref):
    def put(dst, src, size):
        o_ref[:, dst:dst + size] = w_ref[0, :, src:src + size].astype(BF16)

    put(0, 0, A_CONV_CH)
    put(XBC_OFF, W_XBC, B_CONV_CH)
    put(XC_OFF, W_XC, C_WIDTH)
    put(CONV_CH + ZA_OFF, W_ZA, A_WIDTH)
    put(CONV_CH + ZB_OFF, W_ZB, B_WIDTH)
    put(CONV_CH + GC_OFF, W_GC, C_WIDTH)
    lane = lax.broadcasted_iota(jnp.int32, (1, LANE), 1)
    ba = w_ref[0, :, W_BA - BETA_L:W_BA - BETA_L + LANE]
    dt = w_ref[0, :, W_DT - DT_L:W_DT - DT_L + LANE]
    small = jnp.where(lane < DT_L, ba, jnp.where(lane < DT_L + B_HEADS, dt, 0.0))
    o_ref[:, CONV_CH + SM_OFF:CONV_CH + SM_OFF + LANE] = small.astype(BF16)


def _reorder_w_in(w_in, layer, rows=256):
    return pl.pallas_call(
        _reorder_kernel,
        out_shape=jax.ShapeDtypeStruct((D_MODEL, D_IN_R), BF16),
        grid=(D_MODEL // rows,),
        in_specs=[pl.BlockSpec((1, rows, D_IN), lambda i: (layer, i, 0))],
        out_specs=pl.BlockSpec((rows, D_IN_R), lambda i: (i, 0)),
        name="reorder_w_in",
    )(w_in)


def _layer_params(lp, w_in, layer):
    win = _reorder_w_in(w_in, layer)

    def lane_block(*pieces):
        v = jnp.zeros((LANE,), F32)
        for off, val in pieces:
            v = lax.dynamic_update_slice(v, val.astype(F32), (off,))
        return v

    misc = jnp.concatenate([
        lane_block((ALPHA_L, lp['dt_bias_a']), (DT_L, lp['dt_bias_b'])),
        lane_block((ALPHA_L, lp['a_log_a']), (DT_L, lp['a_log_b'])),
        jnp.tile(lp['norm_a_w'], 2),
        jnp.repeat(lp['d_skip_b'], B_HEADDIM),
        lp['norm_b_w'],
        lp['lru_lambda'],
        lp['b_rgate'], lp['b_igate'],
    ])
    misc = jnp.pad(misc, (0, CONV_CH - misc.shape[0]))
    cw = jnp.concatenate([lp['conv_a_w'], lp['conv_b_w'], lp['conv_c_w']], axis=1)
    cb = jnp.concatenate([jnp.zeros((A_CONV_CH,), F32), lp['conv_b_b'], lp['conv_c_b']])
    pv = jnp.concatenate([jnp.pad(lp['norm_mix'], (0, CONV_CH - D_MODEL))[None], cw, cb[None], misc[None],
                          jnp.zeros((1, CONV_CH), F32)], axis=0)

    def block_diag(wb):
        eye = jnp.eye(C_BLOCKS, dtype=wb.dtype)
        return jnp.einsum('ncd,nm->ncmd', wb, eye).reshape(C_WIDTH, C_WIDTH)

    wri = jnp.concatenate([block_diag(lp['w_rgate']), block_diag(lp['w_igate'])], axis=1).astype(BF16)
    return dict(pv=pv, win=win, wri=wri, n1=lp['norm_ffn1'][None], n2=lp['norm_ffn2'][None])


def _trunk(x, states, layers, big, norm_final, *, chunk, tb, bb, tm):
    bsz, l, _ = x.shape
    delta_s, delta_conv, ssd_h, ssd_conv, lru_h, lru_conv = states
    outs = [[] for _ in range(6)]
    for layer in range(DEPTH):
        lw = layers[layer]
        conv0 = jnp.concatenate([delta_conv[layer], ssd_conv[layer], lru_conv[layer]], axis=-1).astype(F32)
        conv0 = jnp.pad(conv0, ((0, 0), (SUBLANE - (CONV_W - 1), 0), (0, 0)))
        x3, s_new, h_new, l_new, c_new = _mixer(
            x, lw['pv'], lw['win'], lw['wri'], big['wout'], lw['n1'], big['g1'], big['u1'], big['d1'],
            _pair_states(delta_s[layer].astype(F32)), _pair_states(ssd_h[layer].astype(F32)),
            lru_h[layer].astype(F32)[:, None, :], conv0, chunk=chunk, tb=tb, bb=bb, layer=layer)
        fin = norm_final[None] if layer == DEPTH - 1 else None
        x = _ffn(x3.reshape(bsz * l, D_MODEL), lw['n2'], big['g2'], big['u2'], big['d2'], fin,
                 tm=tm, layer=layer).reshape(bsz, l, D_MODEL)
        c_new = c_new[:, SUBLANE - (CONV_W - 1):, :]
        outs[0].append(_unpair_states(s_new))
        outs[1].append(c_new[..., :A_CONV_CH])
        outs[2].append(_unpair_states(h_new))
        outs[3].append(c_new[..., XBC_OFF:XBC_OFF + B_CONV_CH])
        outs[4].append(l_new[:, 0, :])
        outs[5].append(c_new[..., XC_OFF:])
    return x, tuple(jnp.stack(o) for o in outs)


def kernel(x_prompt, x_sample, state_delta_s, state_delta_conv, state_ssd_h, state_ssd_conv, state_lru_h, state_lru_conv, norm_ffn1, ffn1_w_gate, ffn1_w_up, ffn1_w_down, norm_mix, w_in, conv_a_w, a_log_a, dt_bias_a, norm_a_w, conv_b_w, conv_b_b, a_log_b, dt_bias_b, d_skip_b, norm_b_w, conv_c_w, conv_c_b, w_rgate, b_rgate, w_igate, b_igate, lru_lambda, w_out, norm_ffn2, ffn2_w_gate, ffn2_w_up, ffn2_w_down, norm_final):
    params = {
        'norm_ffn1': norm_ffn1, 'ffn1_w_gate': ffn1_w_gate, 'ffn1_w_up': ffn1_w_up, 'ffn1_w_down': ffn1_w_down,
        'norm_mix': norm_mix, 'w_in': w_in,
        'conv_a_w': conv_a_w, 'a_log_a': a_log_a, 'dt_bias_a': dt_bias_a, 'norm_a_w': norm_a_w,
        'conv_b_w': conv_b_w, 'conv_b_b': conv_b_b, 'a_log_b': a_log_b, 'dt_bias_b': dt_bias_b,
        'd_skip_b': d_skip_b, 'norm_b_w': norm_b_w,
        'conv_c_w': conv_c_w, 'conv_c_b': conv_c_b, 'w_rgate': w_rgate, 'b_rgate': b_rgate,
        'w_igate': w_igate, 'b_igate': b_igate, 'lru_lambda': lru_lambda,
        'w_out': w_out,
        'norm_ffn2': norm_ffn2, 'ffn2_w_gate': ffn2_w_gate, 'ffn2_w_up': ffn2_w_up, 'ffn2_w_down': ffn2_w_down,
    }
    matmul_weights = dict(wout='w_out', g1='ffn1_w_gate', u1='ffn1_w_up', d1='ffn1_w_down',
                          g2='ffn2_w_gate', u2='ffn2_w_up', d2='ffn2_w_down')
    big = {short: params[name].astype(BF16) for short, name in matmul_weights.items()}
    small = {k: v for k, v in params.items() if k != 'w_in' and k not in matmul_weights.values()}
    layers = [_layer_params({k: v[i] for k, v in small.items()}, w_in, i) for i in range(DEPTH)]
    bp = x_prompt.shape[0]
    zero_states = (
        jnp.zeros((DEPTH, bp, A_HEADS, A_DK, A_DV), F32),
        jnp.zeros((DEPTH, bp, CONV_W - 1, A_CONV_CH), F32),
        jnp.zeros((DEPTH, bp, B_HEADS, B_STATE, B_HEADDIM), F32),
        jnp.zeros((DEPTH, bp, CONV_W - 1, B_CONV_CH), F32),
        jnp.zeros((DEPTH, bp, C_WIDTH), F32),
        jnp.zeros((DEPTH, bp, CONV_W - 1, C_WIDTH), F32),
    )
    y_prompt, p_states = _trunk(x_prompt, zero_states, layers, big, norm_final, chunk=128, tb=256, bb=1, tm=1024)
    sample_states = (state_delta_s, state_delta_conv, state_ssd_h, state_ssd_conv, state_lru_h, state_lru_conv)
    y_sample, s_states = _trunk(x_sample, sample_states, layers, big, norm_final, chunk=64, tb=64, bb=4, tm=512)
    return (y_prompt, y_sample) + tuple(p_states) + tuple(s_states)
```

```python
import functools

import jax
import jax.numpy as jnp
from jax import lax
from jax.experimental import pallas as pl
from jax.experimental.pallas import tpu as pltpu

F32 = jnp.float32
BF16 = jnp.bfloat16

D_MODEL = 1024
DEPTH = 2
D_FF = 2816
EPS = 1e-6
CONV_W = 4
A_HEADS = 6
A_DK = 64
A_DV = 64
A_WIDTH = A_HEADS * A_DV
A_CONV_CH = 2 * A_HEADS * A_DK + A_WIDTH
B_HEADS = 6
B_HEADDIM = 64
B_WIDTH = B_HEADS * B_HEADDIM
B_GROUPS = 2
B_STATE = 64
B_CONV_CH = B_WIDTH + 2 * B_GROUPS * B_STATE
C_WIDTH = 256
C_BLOCKS = 8
C_BLOCK = C_WIDTH // C_BLOCKS
LRU_C = 8.0
D_MIX = A_WIDTH + B_WIDTH + C_WIDTH

LANE = 128
SUBLANE = 8
MXU_TILE = 256
assert D_FF % MXU_TILE == 0
HEAD = 64
N_PAIR = A_HEADS // 2
INV_BASE = 16
OUT_STAGE_THUNKS = 10

CONV_CH = A_CONV_CH + B_CONV_CH + C_WIDTH
K_OFF = A_HEADS * A_DK
V_OFF = 2 * A_HEADS * A_DK
XBC_OFF = A_CONV_CH
BM_OFF = XBC_OFF + B_WIDTH
CM_OFF = BM_OFF + B_GROUPS * B_STATE
XC_OFF = A_CONV_CH + B_CONV_CH
GATE_CH = A_WIDTH + B_WIDTH + C_WIDTH + LANE
ZA_OFF, ZB_OFF, GC_OFF, SM_OFF = 0, A_WIDTH, A_WIDTH + B_WIDTH, A_WIDTH + B_WIDTH + C_WIDTH
D_IN_R = CONV_CH + GATE_CH
BETA_L, ALPHA_L, DT_L = 0, A_HEADS, 2 * A_HEADS
PV_ROWS = 8
PV_NORM, PV_CW, PV_CB, PV_MISC = 0, 1, 5, 6
M_BIAS, M_ALOG, M_NA, M_DSKIP, M_NB, M_LAM, M_BRI = 0, 128, 256, 384, 768, 1152, 1408

VMEM_LIMIT = 56 * 1024 * 1024


def _dot(a, b):
    return jnp.dot(a, b, preferred_element_type=F32)


def _dot_nt(a, b):
    return lax.dot_general(a, b, (((1,), (1,)), ((), ())), preferred_element_type=F32)


def _dot_tn(a, b):
    return lax.dot_general(a, b, (((0,), (0,)), ((), ())), preferred_element_type=F32)


def _split(x, n):
    parts = []
    r = x
    for _ in range(n):
        p = r.astype(BF16)
        parts.append(p)
        r = r - p.astype(F32)
    return parts


def _dot_left_exact(m, x, n):
    return sum(_dot(m, p) for p in _split(x, n))


def _dot_right_exact(x, m, n):
    return sum(_dot(p, m) for p in _split(x, n))


def _sigmoid(x):
    return 0.5 + 0.5 * jnp.tanh(0.5 * x)


def _silu(x):
    h = 0.5 * x
    return h + h * jnp.tanh(h)


def _softplus(x):
    return jnp.maximum(x, 0.0) + jnp.log1p(jnp.exp(-jnp.abs(x)))


def _gelu_tanh(x):
    return 0.5 * x * (1.0 + jnp.tanh(0.7978845608028654 * (x + 0.044715 * x * x * x)))


def _rms(x, w):
    ms = jnp.mean(x * x, axis=-1, keepdims=True)
    return x * lax.rsqrt(ms + EPS) * w


def _ffn_body(x_ref, nw_ref, wg_ref, wu_ref, wd_ref, fnw_ref, o_ref, n_split):
    x = x_ref[...]
    h = _rms(x, nw_ref[...]).astype(BF16)
    tiles = D_FF // MXU_TILE
    bounds = [MXU_TILE * ((tiles * j + n_split - 1) // n_split) for j in range(n_split)] + [D_FF]
    acc = None
    for lo_, hi_ in zip(bounds[:-1], bounds[1:]):
        g = _dot(h, wg_ref[:, lo_:hi_])
        u = _dot(h, wu_ref[:, lo_:hi_])
        a = (_silu(g) * u).astype(BF16)
        d = _dot(a, wd_ref[lo_:hi_, :])
        acc = d if acc is None else acc + d
    y = x + 0.5 * acc
    if fnw_ref is not None:
        y = _rms(y, fnw_ref[...])
    o_ref[...] = y


def _ffn_kernel(x_ref, nw_ref, wg_ref, wu_ref, wd_ref, o_ref, *, n_split):
    _ffn_body(x_ref, nw_ref, wg_ref, wu_ref, wd_ref, None, o_ref, n_split)


def _ffn_final_kernel(x_ref, nw_ref, wg_ref, wu_ref, wd_ref, fnw_ref, o_ref, *, n_split):
    _ffn_body(x_ref, nw_ref, wg_ref, wu_ref, wd_ref, fnw_ref, o_ref, n_split)


def _resident(shape, layer=None):
    if layer is None:
        return pl.BlockSpec(shape, lambda *_: (0,) * len(shape), pipeline_mode=pl.Buffered(1))
    return pl.BlockSpec((None,) + tuple(shape), lambda *_: (layer,) + (0,) * len(shape),
                        pipeline_mode=pl.Buffered(1))


def _ffn(x2d, nw, wg, wu, wd, final_w=None, *, tm, layer):
    t = x2d.shape[0]
    assert t % tm == 0
    row = pl.BlockSpec((tm, D_MODEL), lambda i: (i, 0))
    in_specs = [row, _resident((1, D_MODEL)), _resident((D_MODEL, D_FF), layer), _resident((D_MODEL, D_FF), layer),
                _resident((D_FF, D_MODEL), layer)]
    args = [x2d, nw, wg, wu, wd]
    if final_w is None:
        body = functools.partial(_ffn_kernel, n_split=2)
    else:
        body = functools.partial(_ffn_final_kernel, n_split=2)
        in_specs.append(_resident((1, D_MODEL)))
        args.append(final_w)
    return pl.pallas_call(
        body,
        out_shape=jax.ShapeDtypeStruct((t, D_MODEL), F32),
        grid=(t // tm,),
        in_specs=in_specs,
        out_specs=row,
        compiler_params=pltpu.CompilerParams(dimension_semantics=("arbitrary",), vmem_limit_bytes=VMEM_LIMIT),
        name="ffn_final" if final_w is not None else "ffn",
    )(*args)


def _wide_blockdiag(xw, c):
    if c % LANE == 0:
        z = jnp.zeros((c, c), BF16)
        xb = xw.astype(BF16)
        return jnp.concatenate([jnp.concatenate([xb[:, :c], z], axis=1),
                                jnp.concatenate([z, xb[:, c:]], axis=1)], axis=0)
    first = lax.broadcasted_iota(jnp.int32, (1, 2 * c), 1) < c
    return jnp.concatenate([jnp.where(first, xw, 0.0).astype(BF16), jnp.where(first, 0.0, xw).astype(BF16)], axis=0)


def _head_split_rows(x, lo):
    return jnp.concatenate([jnp.where(lo, x, 0.0), jnp.where(lo, 0.0, x)], axis=0)


def _tri_inv_wide(lws, c, between):
    w = 2 * c
    nb = c // INV_BASE
    ri = lax.broadcasted_iota(jnp.int32, (c, w), 0)
    cj = lax.broadcasted_iota(jnp.int32, (c, w), 1) % c
    lane_w = lax.broadcasted_iota(jnp.int32, (1, w), 1)
    lane_blk = (lane_w % c) // INV_BASE
    blk_rows = [jnp.where(lane_w // INV_BASE == kb, 1.0, 0.0).astype(BF16) for kb in range(2 * nb)]

    def to_slab(x):
        return sum(jnp.where(lane_blk == b, x[b * INV_BASE:(b + 1) * INV_BASE, :], 0.0) for b in range(nb))

    def slab_blockdiag(sl):
        sb = sl.astype(BF16)
        return jnp.concatenate([sb * m for m in blk_rows], axis=0)

    def from_slab(sl):
        return jnp.concatenate([jnp.where(lane_blk == b, sl, 0.0) for b in range(nb)], axis=0)

    eye_slab = jnp.where(lax.broadcasted_iota(jnp.int32, (INV_BASE, w), 0) == lane_w % INV_BASE, 1.0, 0.0)
    ns = [-to_slab(lw) for lw in lws]
    ts = [eye_slab + n for n in ns]
    ps = [_dot(n.astype(BF16), slab_blockdiag(n)) for n in ns]
    between()
    n_pow = 2
    while 2 * n_pow < INV_BASE:
        outs = [_dot(jnp.concatenate([p.astype(BF16), t.astype(BF16)], axis=0), slab_blockdiag(p))
                for p, t in zip(ps, ts)]
        between()
        ts = [t + o[INV_BASE:] for t, o in zip(ts, outs)]
        ps = [o[:INV_BASE] for o in outs]
        n_pow *= 2
    ts = [t + _dot(t.astype(BF16), slab_blockdiag(p)) for t, p in zip(ts, ps)]
    between()
    ts = [from_slab(t) for t in ts]

    def mm(a, bw):
        return _dot(a.astype(BF16), _wide_blockdiag(bw, c))

    s = INV_BASE
    while s < c:
        pairs = range(c // (2 * s))
        odd = lambda x: jnp.concatenate([x[(2 * i + 1) * s:(2 * i + 2) * s] for i in pairs], axis=0)
        sub = (ri // (2 * s) == cj // (2 * s)) & ((ri // s) % 2 == 1) & ((cj // s) % 2 == 0)
        ets = [mm(odd(jnp.where(sub, lw, 0.0)), t) for lw, t in zip(lws, ts)]
        between()
        zero = jnp.zeros((s, w), F32)
        spread = lambda x: jnp.concatenate([y for i in pairs for y in (zero, x[i * s:(i + 1) * s])], axis=0)
        news = [odd(t) - mm(odd(t), spread(et)) for t, et in zip(ts, ets)]
        between()
        ts = [jnp.concatenate([y for i in pairs for y in (t[2 * i * s:(2 * i + 1) * s], nw[i * s:(i + 1) * s])],
                              axis=0) for t, nw in zip(ts, news)]
        s *= 2
    return ts


def _mixer_kernel(x_ref, xn_ref, pv_ref, win_ref, wri_ref, wout_ref, n1_ref, wg_ref, wu_ref, wd_ref,
                  s0_ref, h0_ref, l0_ref, c0_ref,
                  xo_ref, s_ref, hs_ref, l_ref, co_ref,
                  cbuf, act, gates, gact, scal, obuf, mixo, hbuf, hbuf1, gbuf, abuf, facc, x2c, x2n,
                  *, chunk, tb, bb, n_t):
    c = chunk
    nck = tb // c
    t_idx = pl.program_id(1)
    n_piece = pl.cdiv(D_IN_R, MXU_TILE)
    conv_pieces = CONV_CH // MXU_TILE
    ff_tiles = D_FF // MXU_TILE

    def front_pieces(src_ref, x2dst):
        def norm1():
            for sq in range(bb):
                hbuf1[sq * tb:(sq + 1) * tb, :] = _rms(src_ref[sq], n1_ref[...]).astype(BF16)

        def gate(j):
            gbuf[...] = _dot(hbuf1[...], wg_ref[:, j * MXU_TILE:(j + 1) * MXU_TILE])

        def up(j):
            u = _dot(hbuf1[...], wu_ref[:, j * MXU_TILE:(j + 1) * MXU_TILE])
            abuf[j % 2] = (_silu(gbuf[...]) * u).astype(BF16)

        def down(j):
            d = _dot(abuf[j % 2], wd_ref[j * MXU_TILE:(j + 1) * MXU_TILE, :])
            facc[...] = d if j == 0 else facc[...] + d

        def norm2():
            for sq in range(bb):
                x2 = src_ref[sq] + 0.5 * facc[sq * tb:(sq + 1) * tb, :]
                x2dst[sq] = x2
                hbuf[sq * tb:(sq + 1) * tb, :] = _rms(x2, pv_ref[PV_NORM:PV_NORM + 1, 0:D_MODEL]).astype(BF16)

        def piece(j):
            lo_, hi_ = j * MXU_TILE, min((j + 1) * MXU_TILE, D_IN_R)
            res = _dot(hbuf[...], win_ref[:, lo_:hi_])
            for sq in range(bb):
                part = res[sq * tb:(sq + 1) * tb]
                if j < conv_pieces:
                    cbuf[sq, SUBLANE:SUBLANE + tb, lo_:hi_] = part
                else:
                    gates[sq, :, lo_ - CONV_CH:hi_ - CONV_CH] = part

        thunks = [norm1]
        for j in range(ff_tiles):
            thunks += [functools.partial(gate, j), functools.partial(up, j)]
            if j > 0:
                thunks.append(functools.partial(down, j - 1))
        thunks += [functools.partial(down, ff_tiles - 1), norm2]
        return (thunks + [functools.partial(piece, j) for j in range(conv_pieces)],
                [functools.partial(piece, j) for j in range(conv_pieces, n_piece)])

    @pl.when(t_idx == 0)
    def _():
        s_ref[...] = s0_ref[...]
        hs_ref[...] = h0_ref[...]
        l_ref[...] = l0_ref[...]
        cbuf[:, 0:SUBLANE, :] = c0_ref[...]
        for thunk in front_pieces(x_ref, x2c)[0]:
            thunk()

    if n_t > 1:
        @pl.when(t_idx > 0)
        def _():
            x2c[...] = x2n[...]

    ahead, gate_pieces = front_pieces(xn_ref, x2n)
    fillers = gate_pieces + (ahead if n_t > 1 else [])
    n_units = bb * nck * N_PAIR
    conv_hooks, gate_hooks, out_hooks = CONV_CH // LANE, SM_OFF // LANE, 6 * bb
    mid_hooks = (bb * nck + 4 * n_units + 4 + 2 * max(0, (c // INV_BASE).bit_length() - 1)
                 + nck * (2 + bb * N_PAIR))
    assert len(gate_pieces) <= conv_hooks
    left = max(0, len(fillers) - conv_hooks)
    out_n = min(OUT_STAGE_THUNKS, left)
    gate_n = min(gate_hooks // 2, left - out_n)
    rate = [1.0]
    credit = [0.0]

    def fill():
        credit[0] += rate[0]
        while credit[0] >= 1.0 and fillers:
            fillers.pop(0)()
            credit[0] -= 1.0
    misc = pv_ref[PV_MISC:PV_MISC + 1, :]
    lane1 = lax.broadcasted_iota(jnp.int32, (1, LANE), 1)
    lo = lane1 < HEAD
    lo2 = (lax.broadcasted_iota(jnp.int32, (1, 2 * LANE), 1) % LANE) < HEAD
    dec_lanes = (lane1 >= ALPHA_L) & (lane1 < DT_L + B_HEADS)
    nega = jnp.where(dec_lanes, -jnp.exp(misc[:, M_ALOG:M_ALOG + LANE]), 0.0)
    r128 = lax.broadcasted_iota(jnp.int32, (LANE, LANE), 0)
    c128 = lax.broadcasted_iota(jnp.int32, (LANE, LANE), 1)
    same_head = (r128 // HEAD) == (c128 // HEAD)
    ones_head = jnp.where(same_head, 1.0, 0.0).astype(BF16)
    row_lo = r128 < HEAD

    def head_sum(v):
        return _dot_right_exact(v, ones_head, 1)

    row8 = lax.broadcasted_iota(jnp.int32, (SUBLANE, LANE), 0)
    tiles_per_piece = MXU_TILE // LANE
    for j in range(CONV_CH // LANE):
        cs = slice(j * LANE, (j + 1) * LANE)
        for sq in range(bb):
            xr = cbuf[sq, SUBLANE:SUBLANE + tb, cs]
            head = cbuf[sq, 0:SUBLANE, cs]
            acc = xr * pv_ref[PV_CW + CONV_W - 1:PV_CW + CONV_W, cs]
            for d in range(1, CONV_W):
                rolled = pltpu.roll(xr, d, 0)
                top = jnp.where(row8 < d, pltpu.roll(head, d, 0), rolled[0:SUBLANE])
                shifted = jnp.concatenate([top, rolled[SUBLANE:]], axis=0)
                acc = acc + shifted * pv_ref[PV_CW + CONV_W - 1 - d:PV_CW + CONV_W - d, cs]
            if j * LANE >= XBC_OFF:
                acc = acc + pv_ref[PV_CB:PV_CB + 1, cs]
            if j * LANE < XC_OFF:
                acc = _silu(acc)
            if j * LANE < V_OFF:
                acc = acc * lax.rsqrt(head_sum(acc * acc) + EPS)
                if j * LANE < K_OFF:
                    acc = acc * (A_DK ** -0.5)
            act[sq, :, cs] = acc
        if j % tiles_per_piece == tiles_per_piece - 1:
            ps_ = slice((j + 1 - tiles_per_piece) * LANE, (j + 1) * LANE)
            for sq in range(bb):
                tail = cbuf[sq, tb:tb + SUBLANE, ps_]
                co_ref[sq, :, ps_] = tail
                cbuf[sq, 0:SUBLANE, ps_] = tail
        fill()
    assert len(fillers) <= (len(ahead) if n_t > 1 else 0)
    rate[0] = gate_n / gate_hooks
    for j in range(SM_OFF // LANE):
        cs = slice(j * LANE, (j + 1) * LANE)
        for sq in range(bb):
            gv = gates[sq, :, cs]
            gact[sq, :, cs] = _silu(gv) if j * LANE < GC_OFF else _gelu_tanh(gv)
        fill()
    assert n_t == 1 or len(fillers) >= conv_pieces
    rate[0] = max(0, len(fillers) - out_n) / mid_hooks
    credit[0] = 0.0
    for sq in range(bb):
        sm = gates[sq, :, SM_OFF:SM_OFF + LANE]
        sp = _softplus(sm + misc[:, M_BIAS:M_BIAS + LANE])
        scal[sq, 0] = _sigmoid(sm)
        scal[sq, 1] = sp
        scal[sq, 2] = sp * nega

    ri = lax.broadcasted_iota(jnp.int32, (c, 2 * c), 0)
    cjw = lax.broadcasted_iota(jnp.int32, (c, 2 * c), 1)
    first_w = cjw < c
    cj = cjw % c
    causal_w = ri >= cj
    strict_w = ri > cj
    tril = jnp.where(lax.broadcasted_iota(jnp.int32, (c, c), 0) >= lax.broadcasted_iota(jnp.int32, (c, c), 1),
                     1.0, 0.0).astype(BF16)

    items = [(sq, ck) for sq in range(bb) for ck in range(nck)]
    rows = {it: slice(it[1] * c, (it[1] + 1) * c) for it in items}

    sc = {}
    for it in items:
        sq, rs = it[0], rows[it]
        g = _dot_left_exact(tril, scal[sq, 2, rs, :], 2)
        gt2 = jnp.concatenate([g, g], axis=0).T
        glast = g[c - 1:c, :]
        sc[it] = dict(g=g, gt2=gt2, eg=jnp.exp(g), egl=jnp.exp(glast - g), gtot=jnp.exp(glast))
        fill()

    def pairvec(arr, la_, lb_):
        return jnp.where(lo, arr[:, la_:la_ + 1], arr[:, lb_:lb_ + 1])

    def decay_w(d, la_, lb_):
        gcol = jnp.where(first_w, d['g'][:, la_:la_ + 1], d['g'][:, lb_:lb_ + 1])
        grow = jnp.where(first_w[0:1], d['gt2'][la_:la_ + 1, :], d['gt2'][lb_:lb_ + 1, :])
        return jnp.exp(jnp.where(causal_w, gcol - grow, -jnp.inf))

    def state_scale(d, la_, lb_):
        return jnp.where(row_lo, d['gtot'][:, la_:la_ + 1], d['gtot'][:, lb_:lb_ + 1])

    units = [(it, p) for it in items for p in range(N_PAIR)]

    ga = {}
    for un in units:
        (sq, _), p = un
        rs = rows[un[0]]
        qn = act[sq, rs, p * LANE:(p + 1) * LANE]
        kn = act[sq, rs, K_OFF + p * LANE:K_OFF + (p + 1) * LANE]
        ksplit = _head_split_rows(kn, lo).astype(BF16)
        kq = _dot_nt(jnp.concatenate([kn.astype(BF16), qn.astype(BF16)], axis=0), ksplit)
        ga[un] = dict(qn=qn, kn=kn, kq=kq)
        fill()
    lws = []
    for un in units:
        (sq, _), p = un
        rs, d, a = rows[un[0]], sc[un[0]], ga[un]
        ha, hb_ = 2 * p, 2 * p + 1
        gam = decay_w(d, ALPHA_L + ha, ALPHA_L + hb_)
        beta_c = scal[sq, 0, rs, :]
        bcol = jnp.where(first_w, beta_c[:, BETA_L + ha:BETA_L + ha + 1], beta_c[:, BETA_L + hb_:BETA_L + hb_ + 1])
        lws.append(jnp.where(strict_w, bcol * a['kq'][:c] * gam, 0.0))
        a['qk'] = (a['kq'][c:] * gam).astype(BF16)
        bvec = pairvec(beta_c, BETA_L + ha, BETA_L + hb_)
        egv = pairvec(d['eg'], ALPHA_L + ha, ALPHA_L + hb_)
        eglv = pairvec(d['egl'], ALPHA_L + ha, ALPHA_L + hb_)
        vp = act[sq, rs, V_OFF + p * LANE:V_OFF + (p + 1) * LANE]
        rhs = jnp.concatenate([bvec * vp, bvec * egv * a['kn']], axis=1)
        a['rhs'] = _head_split_rows(rhs, lo2).astype(BF16)
        a['qd'] = a['qn'] * egv
        a['kd'] = (a['kn'] * eglv).astype(BF16)
        del a['kq']
        fill()

    gb = {}
    for it in items:
        sq, rs = it[0], rows[it]
        bm = act[sq, rs, BM_OFF:BM_OFF + LANE]
        cm = act[sq, rs, CM_OFF:CM_OFF + LANE]
        bm_sw = pltpu.roll(bm, HEAD, 1)
        cm_sw = pltpu.roll(cm, HEAD, 1)
        for p in range(N_PAIR):
            ha, hb_ = 2 * p, 2 * p + 1
            g0, g1 = ha // (B_HEADS // B_GROUPS), hb_ // (B_HEADS // B_GROUPS)
            if g0 == g1 == 0:
                bsel, csel = jnp.where(lo, bm, bm_sw), jnp.where(lo, cm, cm_sw)
            elif g0 == g1 == 1:
                bsel, csel = jnp.where(lo, bm_sw, bm), jnp.where(lo, cm_sw, cm)
            else:
                bsel, csel = bm, cm
            gb[(it, p)] = dict(bsel=bsel, csel=csel,
                               scores=_dot_nt(csel.astype(BF16), _head_split_rows(bsel, lo).astype(BF16)))
            fill()
    for un in units:
        (sq, _), p = un
        rs, d, b = rows[un[0]], sc[un[0]], gb[un]
        ha, hb_ = 2 * p, 2 * p + 1
        xs = act[sq, rs, XBC_OFF + p * LANE:XBC_OFF + (p + 1) * LANE]
        xdt = xs * pairvec(scal[sq, 1, rs, :], DT_L + ha, DT_L + hb_)
        b['xb'] = xdt.astype(BF16)
        mw = (b['scores'] * decay_w(d, DT_L + ha, DT_L + hb_)).astype(BF16)
        b['y'] = _dot(mw, _head_split_rows(xdt, lo).astype(BF16))
        b['cd'] = (b['csel'] * pairvec(d['eg'], DT_L + ha, DT_L + hb_)).astype(BF16)
        b['bd'] = (b['bsel'] * pairvec(d['egl'], DT_L + ha, DT_L + hb_)).astype(BF16)
        b['xs'] = xs
        del b['scores'], b['csel'], b['bsel']
        fill()

    tws = _tri_inv_wide(lws, c, fill)
    for un, tw in zip(units, tws):
        ga[un]['uw'] = _dot(tw.astype(BF16), ga[un]['rhs'])

    for ck in range(nck):
        cur = [((sq, ck), p) for sq in range(bb) for p in range(N_PAIR)]
        s_old = {un: s_ref[un[0][0], un[1]] for un in cur}
        h_old = {un: hs_ref[un[0][0], un[1]] for un in cur}
        wq = {un: _dot(jnp.concatenate([ga[un]['uw'][:, LANE:].astype(BF16), ga[un]['qd'].astype(BF16)], axis=0),
                       s_old[un].astype(BF16)) for un in cur}
        fill()
        yi = {un: _dot(gb[un]['cd'], h_old[un].astype(BF16)) for un in cur}
        fill()
        for un in cur:
            (sq, _), p = un
            rs, d, a, b = rows[un[0]], sc[un[0]], ga[un], gb[un]
            ha, hb_ = 2 * p, 2 * p + 1
            delta = a['uw'][:, :LANE] - wq[un][:c]
            o = wq[un][c:] + _dot(a['qk'], _head_split_rows(delta, lo).astype(BF16))
            upd = _dot_tn(a['kd'], delta.astype(BF16))
            s_ref[sq, p] = state_scale(d, ALPHA_L + ha, ALPHA_L + hb_) * s_old[un] + jnp.where(same_head, upd, 0.0)
            obuf[sq, rs, p * LANE:(p + 1) * LANE] = o
            updh = _dot_tn(b['bd'], b['xb'])
            hs_ref[sq, p] = state_scale(d, DT_L + ha, DT_L + hb_) * h_old[un] + jnp.where(same_head, updh, 0.0)
            y = b['y'] + yi[un] + misc[:, M_DSKIP + p * LANE:M_DSKIP + (p + 1) * LANE] * b['xs']
            obuf[sq, rs, A_WIDTH + p * LANE:A_WIDTH + (p + 1) * LANE] = (
                y * gact[sq, rs, ZB_OFF + p * LANE:ZB_OFF + (p + 1) * LANE])
            fill()

    g_width = B_WIDTH // B_GROUPS
    assert B_GROUPS == 2
    g3r =lax.broadcasted_iota(jnp.int32, (B_WIDTH, LANE), 0) // g_width
    g3c = lax.broadcasted_iota(jnp.int32, (B_WIDTH, LANE), 1) // HEAD
    ones_group = jnp.where(g3r == g3c, 1.0, 0.0).astype(BF16)
    in_group0 = lax.broadcasted_iota(jnp.int32, (1, B_WIDTH), 1) < g_width
    rowt = lax.broadcasted_iota(jnp.int32, (tb, LANE), 0)
    rate[0] = len(fillers) / out_hooks
    credit[0] = 0.0
    for sq in range(bb):
        ms_ = slice(sq * tb, (sq + 1) * tb)
        for p in range(N_PAIR):
            o = obuf[sq, :, p * LANE:(p + 1) * LANE]
            ms = head_sum(o * o) * (1.0 / A_DV)
            on = o * lax.rsqrt(ms + EPS) * misc[:, M_NA:M_NA + LANE]
            mixo[ms_, p * LANE:(p + 1) * LANE] = (
                on * gact[sq, :, ZA_OFF + p * LANE:ZA_OFF + (p + 1) * LANE]).astype(BF16)
            fill()
        yb = obuf[sq, :, A_WIDTH:A_WIDTH + B_WIDTH]
        gs = _dot_right_exact(yb * yb, ones_group, 1)
        ms = jnp.where(in_group0, gs[:, 0:1], gs[:, HEAD:HEAD + 1]) * (1.0 / g_width)
        ob = yb * lax.rsqrt(ms + EPS) * misc[:, M_NB:M_NB + B_WIDTH]
        mixo[ms_, A_WIDTH:A_WIDTH + B_WIDTH] = ob.astype(BF16)
        fill()

        xc = act[sq, :, XC_OFF:XC_OFF + C_WIDTH]
        rig = _dot(xc.astype(BF16), wri_ref[...]) + misc[:, M_BRI:M_BRI + 2 * C_WIDTH]
        for half in range(C_WIDTH // LANE):
            hs_ = slice(half * LANE, (half + 1) * LANE)
            xch = xc[:, hs_]
            lam = misc[:, M_LAM + half * LANE:M_LAM + (half + 1) * LANE]
            log_a = -LRU_C * _sigmoid(rig[:, hs_]) * _softplus(-lam)
            a = jnp.exp(log_a)
            b = jnp.sqrt(1.0 - jnp.exp(2.0 * log_a)) * (
                _sigmoid(rig[:, C_WIDTH + half * LANE:C_WIDTH + (half + 1) * LANE]) * xch)
            d = 1
            while d < tb:
                keep = rowt >= d
                a_sh = jnp.where(keep, pltpu.roll(a, d, 0), 1.0)
                b_sh = jnp.where(keep, pltpu.roll(b, d, 0), 0.0)
                b = a * b_sh + b
                a = a * a_sh
                d *= 2
            hseq = a * l_ref[sq, :, hs_] + b
            l_ref[sq, :, hs_] = hseq[tb - 1:tb, :]
            mixo[ms_, A_WIDTH + B_WIDTH + half * LANE:A_WIDTH + B_WIDTH + (half + 1) * LANE] = (
                hseq * gact[sq, :, GC_OFF + half * LANE:GC_OFF + (half + 1) * LANE]).astype(BF16)
            fill()

    while fillers:
        fillers.pop(0)()
    mix = _dot(mixo[...], wout_ref[...])
    for sq in range(bb):
        xo_ref[sq] = x2c[sq] + mix[sq * tb:(sq + 1) * tb]


def _mixer(x, pv, win, wri, wout, n1, wg, wu, wd, s0, h0, l0, c0, *, chunk, tb, bb, layer):
    bsz, l, _ = x.shape
    assert l % tb == 0 and tb % chunk == 0 and bsz % bb == 0
    n_t = l // tb
    seq = lambda b, t: (b, 0, 0)
    seq4 = lambda b, t: (b, 0, 0, 0)
    in_specs = [
        pl.BlockSpec((bb, tb, D_MODEL), lambda b, t: (b, t, 0)),
        pl.BlockSpec((bb, tb, D_MODEL), lambda b, t: (b, jnp.minimum(t + 1, n_t - 1), 0)),
        _resident((PV_ROWS, CONV_CH)),
        _resident((D_MODEL, D_IN_R)),
        _resident((C_WIDTH, 2 * C_WIDTH)),
        _resident((D_MIX, D_MODEL), layer),
        _resident((1, D_MODEL)),
        _resident((D_MODEL, D_FF), layer),
        _resident((D_MODEL, D_FF), layer),
        _resident((D_FF, D_MODEL), layer),
        pl.BlockSpec((bb, N_PAIR, LANE, LANE), seq4),
        pl.BlockSpec((bb, N_PAIR, LANE, LANE), seq4),
        pl.BlockSpec((bb, 1, C_WIDTH), seq),
        pl.BlockSpec((bb, SUBLANE, CONV_CH), seq),
    ]
    out_specs = [
        pl.BlockSpec((bb, tb, D_MODEL), lambda b, t: (b, t, 0)),
        pl.BlockSpec((bb, N_PAIR, LANE, LANE), seq4),
        pl.BlockSpec((bb, N_PAIR, LANE, LANE), seq4),
        pl.BlockSpec((bb, 1, C_WIDTH), seq),
        pl.BlockSpec((bb, SUBLANE, CONV_CH), seq),
    ]
    out_shape = [
        jax.ShapeDtypeStruct((bsz, l, D_MODEL), F32),
        jax.ShapeDtypeStruct((bsz, N_PAIR, LANE, LANE), F32),
        jax.ShapeDtypeStruct((bsz, N_PAIR, LANE, LANE), F32),
        jax.ShapeDtypeStruct((bsz, 1, C_WIDTH), F32),
        jax.ShapeDtypeStruct((bsz, SUBLANE, CONV_CH), F32),
    ]
    scratch = [
        pltpu.VMEM((bb, tb + SUBLANE, CONV_CH), F32),
        pltpu.VMEM((bb, tb, CONV_CH), F32),
        pltpu.VMEM((bb, tb, GATE_CH), F32),
        pltpu.VMEM((bb, tb, SM_OFF), F32),
        pltpu.VMEM((bb, 3, tb, LANE), F32),
        pltpu.VMEM((bb, tb, A_WIDTH + B_WIDTH), F32),
        pltpu.VMEM((bb * tb, D_MIX), BF16),
        pltpu.VMEM((bb * tb, D_MODEL), BF16),
        pltpu.VMEM((bb * tb, D_MODEL), BF16),
        pltpu.VMEM((bb * tb, MXU_TILE), F32),
        pltpu.VMEM((2, bb * tb, MXU_TILE), BF16),
        pltpu.VMEM((bb * tb, D_MODEL), F32),
        pltpu.VMEM((bb, tb, D_MODEL), F32),
        pltpu.VMEM((bb, tb, D_MODEL), F32),
    ]
    return pl.pallas_call(
        functools.partial(_mixer_kernel, chunk=chunk, tb=tb, bb=bb, n_t=n_t),
        out_shape=out_shape,
        grid=(bsz // bb, l // tb),
        in_specs=in_specs,
        out_specs=out_specs,
        scratch_shapes=scratch,
        compiler_params=pltpu.CompilerParams(dimension_semantics=("arbitrary", "arbitrary"),
                                             vmem_limit_bytes=VMEM_LIMIT),
        name="mixer_c%d" % chunk,
    )(x, x, pv, win, wri, wout, n1, wg, wu, wd, s0, h0, l0, c0)


def _pair_states(s):
    bsz = s.shape[0]
    s = s.reshape(bsz, N_PAIR, 2, HEAD, HEAD)
    z = jnp.zeros((bsz, N_PAIR, HEAD, HEAD), s.dtype)
    top = jnp.concatenate([s[:, :, 0], z], axis=-1)
    bot = jnp.concatenate([z, s[:, :, 1]], axis=-1)
    return jnp.concatenate([top, bot], axis=-2)


def _unpair_states(sp):
    a = sp[:, :, :HEAD, :HEAD]
    b = sp[:, :, HEAD:, HEAD:]
    return jnp.stack([a, b], axis=2).reshape(sp.shape[0], A_HEADS, HEAD, HEAD)


W_ZA = A_CONV_CH
W_BA = W_ZA + A_WIDTH
W_AA = W_BA + A_HEADS
W_ZB = W_AA + A_HEADS
W_XBC = W_ZB + B_WIDTH
W_DT = W_XBC + B_CONV_CH
W_GC = W_DT + B_HEADS
W_XC = W_GC + C_WIDTH
D_IN = W_XC + C_WIDTH
assert W_BA % LANE == BETA_L and W_AA % LANE == ALPHA_L and W_DT % LANE == DT_L


def _reorder_kernel(w_ref, o_ref):
    def put(dst, src, size):
        o_ref[:, dst:dst + size] = w_ref[0, :, src:src + size].astype(BF16)

    put(0, 0, A_CONV_CH)
    put(XBC_OFF, W_XBC, B_CONV_CH)
    put(XC_OFF, W_XC, C_WIDTH)
    put(CONV_CH + ZA_OFF, W_ZA, A_WIDTH)
    put(CONV_CH + ZB_OFF, W_ZB, B_WIDTH)
    put(CONV_CH + GC_OFF, W_GC, C_WIDTH)
    lane = lax.broadcasted_iota(jnp.int32, (1, LANE), 1)
    ba = w_ref[0, :, W_BA - BETA_L:W_BA - BETA_L + LANE]
    dt = w_ref[0, :, W_DT - DT_L:W_DT - DT_L + LANE]
    small = jnp.where(lane < DT_L, ba, jnp.where(lane < DT_L + B_HEADS, dt, 0.0))
    o_ref[:, CONV_CH + SM_OFF:CONV_CH + SM_OFF + LANE] = small.astype(BF16)


def _reorder_w_in(w_in, layer, rows=256):
    return pl.pallas_call(
        _reorder_kernel,
        out_shape=jax.ShapeDtypeStruct((D_MODEL, D_IN_R), BF16),
        grid=(D_MODEL // rows,),
        in_specs=[pl.BlockSpec((1, rows, D_IN), lambda i: (layer, i, 0))],
        out_specs=pl.BlockSpec((rows, D_IN_R), lambda i: (i, 0)),
        name="reorder_w_in",
    )(w_in)


def _layer_params(lp, w_in, layer):
    win = _reorder_w_in(w_in, layer)

    def lane_block(*pieces):
        v = jnp.zeros((LANE,), F32)
        for off, val in pieces:
            v = lax.dynamic_update_slice(v, val.astype(F32), (off,))
        return v

    misc = jnp.concatenate([
        lane_block((ALPHA_L, lp['dt_bias_a']), (DT_L, lp['dt_bias_b'])),
        lane_block((ALPHA_L, lp['a_log_a']), (DT_L, lp['a_log_b'])),
        jnp.tile(lp['norm_a_w'], 2),
        jnp.repeat(lp['d_skip_b'], B_HEADDIM),
        lp['norm_b_w'],
        lp['lru_lambda'],
        lp['b_rgate'], lp['b_igate'],
    ])
    misc = jnp.pad(misc, (0, CONV_CH - misc.shape[0]))
    cw = jnp.concatenate([lp['conv_a_w'], lp['conv_b_w'], lp['conv_c_w']], axis=1)
    cb = jnp.concatenate([jnp.zeros((A_CONV_CH,), F32), lp['conv_b_b'], lp['conv_c_b']])
    pv = jnp.concatenate([jnp.pad(lp['norm_mix'], (0, CONV_CH - D_MODEL))[None], cw, cb[None], misc[None],
                          jnp.zeros((1, CONV_CH), F32)], axis=0)

    def block_diag(wb):
        eye = jnp.eye(C_BLOCKS, dtype=wb.dtype)
        return jnp.einsum('ncd,nm->ncmd', wb, eye).reshape(C_WIDTH, C_WIDTH)

    wri = jnp.concatenate([block_diag(lp['w_rgate']), block_diag(lp['w_igate'])], axis=1).astype(BF16)
    return dict(pv=pv, win=win, wri=wri, n1=lp['norm_ffn1'][None], n2=lp['norm_ffn2'][None])


def _trunk(x, states, layers, big, norm_final, *, chunk, tb, bb, tm):
    bsz, l, _ = x.shape
    delta_s, delta_conv, ssd_h, ssd_conv, lru_h, lru_conv = states
    outs = [[] for _ in range(6)]
    for layer in range(DEPTH):
        lw = layers[layer]
        conv0 = jnp.concatenate([delta_conv[layer], ssd_conv[layer], lru_conv[layer]], axis=-1).astype(F32)
        conv0 = jnp.pad(conv0, ((0, 0), (SUBLANE - (CONV_W - 1), 0), (0, 0)))
        x3, s_new, h_new, l_new, c_new = _mixer(
            x, lw['pv'], lw['win'], lw['wri'], big['wout'], lw['n1'], big['g1'], big['u1'], big['d1'],
            _pair_states(delta_s[layer].astype(F32)), _pair_states(ssd_h[layer].astype(F32)),
            lru_h[layer].astype(F32)[:, None, :], conv0, chunk=chunk, tb=tb, bb=bb, layer=layer)
        fin = norm_final[None] if layer == DEPTH - 1 else None
        x = _ffn(x3.reshape(bsz * l, D_MODEL), lw['n2'], big['g2'], big['u2'], big['d2'], fin,
                 tm=tm, layer=layer).reshape(bsz, l, D_MODEL)
        c_new = c_new[:, SUBLANE - (CONV_W - 1):, :]
        outs[0].append(_unpair_states(s_new))
        outs[1].append(c_new[..., :A_CONV_CH])
        outs[2].append(_unpair_states(h_new))
        outs[3].append(c_new[..., XBC_OFF:XBC_OFF + B_CONV_CH])
        outs[4].append(l_new[:, 0, :])
        outs[5].append(c_new[..., XC_OFF:])
    return x, tuple(jnp.stack(o) for o in outs)


def kernel(x_prompt, x_sample, state_delta_s, state_delta_conv, state_ssd_h, state_ssd_conv, state_lru_h, state_lru_conv, norm_ffn1, ffn1_w_gate, ffn1_w_up, ffn1_w_down, norm_mix, w_in, conv_a_w, a_log_a, dt_bias_a, norm_a_w, conv_b_w, conv_b_b, a_log_b, dt_bias_b, d_skip_b, norm_b_w, conv_c_w, conv_c_b, w_rgate, b_rgate, w_igate, b_igate, lru_lambda, w_out, norm_ffn2, ffn2_w_gate, ffn2_w_up, ffn2_w_down, norm_final):
    params = {
        'norm_ffn1': norm_ffn1, 'ffn1_w_gate': ffn1_w_gate, 'ffn1_w_up': ffn1_w_up, 'ffn1_w_down': ffn1_w_down,
        'norm_mix': norm_mix, 'w_in': w_in,
        'conv_a_w': conv_a_w, 'a_log_a': a_log_a, 'dt_bias_a': dt_bias_a, 'norm_a_w': norm_a_w,
        'conv_b_w': conv_b_w, 'conv_b_b': conv_b_b, 'a_log_b': a_log_b, 'dt_bias_b': dt_bias_b,
        'd_skip_b': d_skip_b, 'norm_b_w': norm_b_w,
        'conv_c_w': conv_c_w, 'conv_c_b': conv_c_b, 'w_rgate': w_rgate, 'b_rgate': b_rgate,
        'w_igate': w_igate, 'b_igate': b_igate, 'lru_lambda': lru_lambda,
        'w_out': w_out,
        'norm_ffn2': norm_ffn2, 'ffn2_w_gate': ffn2_w_gate, 'ffn2_w_up': ffn2_w_up, 'ffn2_w_down': ffn2_w_down,
    }
    matmul_weights = dict(wout='w_out', g1='ffn1_w_gate', u1='ffn1_w_up', d1='ffn1_w_down',
                          g2='ffn2_w_gate', u2='ffn2_w_up', d2='ffn2_w_down')
    big = {short: params[name].astype(BF16) for short, name in matmul_weights.items()}
    small = {k: v for k, v in params.items() if k != 'w_in' and k not in matmul_weights.values()}
    layers = [_layer_params({k: v[i] for k, v in small.items()}, w_in, i) for i in range(DEPTH)]
    bp = x_prompt.shape[0]
    zero_states = (
        jnp.zeros((DEPTH, bp, A_HEADS, A_DK, A_DV), F32),
        jnp.zeros((DEPTH, bp, CONV_W - 1, A_CONV_CH), F32),
        jnp.zeros((DEPTH, bp, B_HEADS, B_STATE, B_HEADDIM), F32),
        jnp.zeros((DEPTH, bp, CONV_W - 1, B_CONV_CH), F32),
        jnp.zeros((DEPTH, bp, C_WIDTH), F32),
        jnp.zeros((DEPTH, bp, CONV_W - 1, C_WIDTH), F32),
    )
    y_prompt, p_states = _trunk(x_prompt, zero_states, layers, big, norm_final, chunk=128, tb=256, bb=1, tm=1024)
    sample_states = (state_delta_s, state_delta_conv, state_ssd_h, state_ssd_conv, state_lru_h, state_lru_conv)
    y_sample, s_states = _trunk(x_sample, sample_states, layers, big, norm_final, chunk=64, tb=64, bb=4, tm=512)
    return (y_prompt, y_sample) + tuple(p_states) + tuple(s_states)
```
